```python
import jax, jax.numpy as jnp
from jax import lax
import numpy as np

D_MODEL = 1024
BATCH = 2
SEQ = 16384
DEPTH = 4

CTX_LEN = 256
GRID_W = 64
N_MIXERS = 2
FFN_DIM = 2816
N_FOURIER_GROUPS = 4
FOURIER_GROUP_DIM = D_MODEL // N_FOURIER_GROUPS
RET_HEADS = 4
RET_QK_DIM = D_MODEL // RET_HEADS
RET_V_DIM = 2 * D_MODEL // RET_HEADS
RET_QK_TOTAL = RET_HEADS * RET_QK_DIM
RET_V_TOTAL = RET_HEADS * RET_V_DIM
RET_IN_DIM = 2 * RET_QK_TOTAL + 2 * RET_V_TOTAL
RET_CHUNK = 128
ROPE_BASE = 10000.0
N_MOD = 9
NORM_EPS = 1e-6
N_FOURIER_LAYERS = (DEPTH + N_MIXERS - 1) // N_MIXERS
N_RET_LAYERS = DEPTH // N_MIXERS

kernel_name = "hybrid_fourier_retention_prefix_dit"


def rmsnorm(x, g):
    xf = x.astype(jnp.float32)
    y = xf * lax.rsqrt(jnp.mean(xf * xf, axis=-1, keepdims=True) + NORM_EPS)
    return (y * g.astype(jnp.float32)).astype(x.dtype)


def modulate(h, shift, scale):
    return h * (1 + scale) + shift


def swiglu(h, w_gate, w_up, w_down):
    return (jax.nn.silu(h @ w_gate) * (h @ w_up)) @ w_down


def fourier_mix(h):
    b, n, d = h.shape
    hg = h.reshape(b, n, N_FOURIER_GROUPS, FOURIER_GROUP_DIM).astype(jnp.float32)
    f = jnp.fft.fft2(hg, axes=(1, 3), norm="ortho").real
    return f.reshape(b, n, d).astype(h.dtype)


def fourier_layer(h_lat, h_ctx, w, bias):
    y_lat = fourier_mix(h_lat) @ w + bias
    y_ctx = None if h_ctx is None else fourier_mix(h_ctx) @ w + bias
    return y_lat, y_ctx


def axial_rope(n_tokens):
    rows = n_tokens // GRID_W
    row = jnp.repeat(jnp.arange(rows, dtype=jnp.float32), GRID_W)
    col = jnp.tile(jnp.arange(GRID_W, dtype=jnp.float32), rows)
    n_freq = RET_QK_DIM // 4
    inv_freq = ROPE_BASE ** (-jnp.arange(n_freq, dtype=jnp.float32) / n_freq)
    ang = jnp.concatenate([row[:, None] * inv_freq, col[:, None] * inv_freq], axis=-1)
    return jnp.cos(ang), jnp.sin(ang)


def apply_rope(t, cos, sin):
    t1, t2 = jnp.split(t, 2, axis=-1)
    cos = cos.astype(t.dtype)
    sin = sin.astype(t.dtype)
    return jnp.concatenate([t1 * cos - t2 * sin, t2 * cos + t1 * sin], axis=-1)


def ret_project(h, w_in):
    b, n, _ = h.shape
    p = h @ w_in
    q, k, v, g = jnp.split(p, [RET_QK_TOTAL, 2 * RET_QK_TOTAL, 2 * RET_QK_TOTAL + RET_V_TOTAL], axis=-1)
    q = q.reshape(b, n, RET_HEADS, RET_QK_DIM).transpose(0, 2, 1, 3)
    k = k.reshape(b, n, RET_HEADS, RET_QK_DIM).transpose(0, 2, 1, 3) * (RET_QK_DIM ** -0.5)
    v = v.reshape(b, n, RET_HEADS, RET_V_DIM).transpose(0, 2, 1, 3)
    return q, k, v, g


def retention_scan(q, k, v, lg, s0):
    b, h, n, dk = q.shape
    dv = v.shape[-1]
    nc = n // RET_CHUNK
    dt = q.dtype
    pos = jnp.arange(RET_CHUNK, dtype=jnp.float32)
    diff = pos[:, None] - pos[None, :]
    intra = jnp.where(diff >= 0, jnp.exp(lg[:, None, None] * jnp.maximum(diff, 0.0)), 0.0).astype(dt)
    q_dec = jnp.exp(lg[:, None] * (pos + 1.0))[:, :, None].astype(dt)
    k_dec = jnp.exp(lg[:, None] * (RET_CHUNK - 1.0 - pos))[:, :, None].astype(dt)
    chunk_dec = jnp.exp(lg * RET_CHUNK)[:, None, None].astype(dt)

    def to_chunks(t):
        return jnp.moveaxis(t.reshape(b, h, nc, RET_CHUNK, t.shape[-1]), 2, 0)

    def step(s, qkv):
        qc, kc, vc = qkv
        scores = jnp.einsum('bhid,bhjd->bhij', qc, kc) * intra
        y = (jnp.einsum('bhij,bhje->bhie', scores, vc)
             + jnp.einsum('bhid,bhde->bhie', qc * q_dec, s))
        s = chunk_dec * s + jnp.einsum('bhjd,bhje->bhde', kc * k_dec, vc)
        return s, y

    s_final, y = lax.scan(step, s0, (to_chunks(q), to_chunks(k), to_chunks(v)))
    y = jnp.moveaxis(y, 0, 2).reshape(b, h, n, dv)
    return y, s_final


def bidirectional_retention(q, k, v, lg_fwd, lg_bwd, s_fwd, s_bwd):
    y_f, s_f = retention_scan(q, k, v, lg_fwd, s_fwd)
    y_b_rev, s_b = retention_scan(jnp.flip(q, 2), jnp.flip(k, 2), jnp.flip(v, 2), lg_bwd, s_bwd)
    return y_f + jnp.flip(y_b_rev, 2), s_f, s_b


def ret_output(y, g, w_out):
    b, h, n, dv = y.shape
    yf = y.astype(jnp.float32)
    mu = jnp.mean(yf, axis=-1, keepdims=True)
    var = jnp.mean(jnp.square(yf - mu), axis=-1, keepdims=True)
    yn = ((yf - mu) * lax.rsqrt(var + NORM_EPS)).astype(g.dtype)
    yn = yn.transpose(0, 2, 1, 3).reshape(b, n, h * dv)
    return (jax.nn.silu(g) * yn) @ w_out


def retention_layer(h_lat, h_ctx, w_in, w_out, decay, cos, sin, need_ctx_out):
    lg = -jnp.abs(decay.astype(jnp.float32))
    qc, kc, vc, gc = ret_project(h_ctx, w_in)
    zeros = jnp.zeros((h_ctx.shape[0], RET_HEADS, RET_QK_DIM, RET_V_DIM), vc.dtype)
    yc, s_f, s_b = bidirectional_retention(qc, kc, vc, lg[0], lg[1], zeros, zeros)
    ql, kl, vl, gl = ret_project(h_lat, w_in)
    ql = apply_rope(ql, cos, sin)
    kl = apply_rope(kl, cos, sin)
    yl, _, _ = bidirectional_retention(ql, kl, vl, lg[0], lg[1], s_f, s_b)
    y_lat = ret_output(yl, gl, w_out)
    y_ctx = ret_output(yc, gc, w_out) if need_ctx_out else None
    return y_lat, y_ctx


def setup_inputs(seed: int = 0) -> dict:
    key = jax.random.key(seed)
    ks = jax.random.split(key, 16)
    f32 = jnp.float32
    base_rate = -np.log(1.0 - 2.0 ** (-5.0 - np.arange(RET_HEADS, dtype=np.float32)))
    base_rate = jnp.asarray(base_rate, dtype=f32)
    return {
        "x": jax.random.normal(ks[0], (BATCH, SEQ, D_MODEL), f32),
        "c": jax.random.normal(ks[1], (BATCH, D_MODEL), f32),
        "ctx": jax.random.normal(ks[2], (BATCH, CTX_LEN, D_MODEL), f32),
        "c_ctx": jax.random.normal(ks[3], (D_MODEL,), f32),
        "ada_w": jax.random.normal(ks[4], (DEPTH, D_MODEL, N_MOD * D_MODEL), f32) * (0.5 * D_MODEL ** -0.5),
        "ada_b": jax.random.normal(ks[5], (DEPTH, N_MOD * D_MODEL), f32) * 0.01,
        "norm_g": 1.0 + 0.02 * jax.random.normal(ks[6], (DEPTH, 3, D_MODEL), f32),
        "final_g": 1.0 + 0.02 * jax.random.normal(ks[7], (D_MODEL,), f32),
        "ffn_w_gate": jax.random.normal(ks[8], (DEPTH, 2, D_MODEL, FFN_DIM), f32) * D_MODEL ** -0.5,
        "ffn_w_up": jax.random.normal(ks[9], (DEPTH, 2, D_MODEL, FFN_DIM), f32) * D_MODEL ** -0.5,
        "ffn_w_down": jax.random.normal(ks[10], (DEPTH, 2, FFN_DIM, D_MODEL), f32) * FFN_DIM ** -0.5,
        "four_w": jax.random.normal(ks[11], (N_FOURIER_LAYERS, D_MODEL, D_MODEL), f32) * D_MODEL ** -0.5,
        "four_b": jax.random.normal(ks[12], (N_FOURIER_LAYERS, D_MODEL), f32) * 0.01,
        "ret_w_in": jax.random.normal(ks[13], (N_RET_LAYERS, D_MODEL, RET_IN_DIM), f32) * D_MODEL ** -0.5,
        "ret_w_out": jax.random.normal(ks[14], (N_RET_LAYERS, RET_V_TOTAL, D_MODEL), f32) * RET_V_TOTAL ** -0.5,
        "ret_decay": base_rate * (1.0 + 0.05 * jax.random.normal(ks[15], (N_RET_LAYERS, 2, RET_HEADS), f32)),
    }


def reference(x, c, ctx, c_ctx, ada_w, ada_b, norm_g, final_g, ffn_w_gate, ffn_w_up, ffn_w_down,
              four_w, four_b, ret_w_in, ret_w_out, ret_decay):
    cos, sin = axial_rope(x.shape[1])
    silu_c = jax.nn.silu(c)
    silu_cc = jax.nn.silu(c_ctx)
    for l in range(DEPTH):
        last = l == DEPTH - 1
        mixer = l % N_MIXERS
        slot = l // N_MIXERS
        ctx_read = (not last) or (mixer == 1)
        m_lat = jnp.split((silu_c @ ada_w[l] + ada_b[l])[:, None, :], N_MOD, axis=-1)
        m_ctx = jnp.split(silu_cc @ ada_w[l] + ada_b[l], N_MOD, axis=-1)

        def half_ffn(t, m, j):
            h = modulate(rmsnorm(t, norm_g[l, j]), m[3 * j], m[3 * j + 1])
            return t + 0.5 * m[3 * j + 2] * swiglu(h, ffn_w_gate[l, j // 2], ffn_w_up[l, j // 2],
                                                  ffn_w_down[l, j // 2])

        x = half_ffn(x, m_lat, 0)
        if ctx_read:
            ctx = half_ffn(ctx, m_ctx, 0)

        h_lat = modulate(rmsnorm(x, norm_g[l, 1]), m_lat[3], m_lat[4])
        h_ctx = modulate(rmsnorm(ctx, norm_g[l, 1]), m_ctx[3], m_ctx[4]) if ctx_read else None
        if mixer == 0:
            y_lat, y_ctx = fourier_layer(h_lat, h_ctx, four_w[slot], four_b[slot])
        else:
            y_lat, y_ctx = retention_layer(h_lat, h_ctx, ret_w_in[slot], ret_w_out[slot], ret_decay[slot],
                                           cos, sin, not last)
        x = x + m_lat[5] * y_lat
        if not last:
            ctx = ctx + m_ctx[5] * y_ctx

        x = half_ffn(x, m_lat, 2)
        if not last:
            ctx = half_ffn(ctx, m_ctx, 2)
    return rmsnorm(x, final_g)
```

```python
import functools

import jax
import jax.numpy as jnp
import numpy as np
from jax import lax
from jax.experimental import pallas as pl
from jax.experimental.pallas import tpu as pltpu

F32 = jnp.float32
BF16 = jnp.bfloat16

GRID_W = 64
N_GROUPS = 4
HEADS = 4
CHUNK = 128
ROPE_BASE = 10000.0
N_MOD = 9
EPS = 1e-6
LANES = 128
MOD_ROWS = 8


def _dot(a, b):
    return jnp.dot(a, b, preferred_element_type=F32)


def _rms(x, g):
    return x * lax.rsqrt(jnp.mean(x * x, axis=-1, keepdims=True) + EPS) * g


def _prenorm_mod(x, g, shift, scale):
    return _rms(x, g) * (1.0 + scale) + shift


def _silu(x):
    return x * jax.nn.sigmoid(x)


def _resident(shape):
    return pl.BlockSpec(shape, lambda *_: (0,) * len(shape), pipeline_mode=pl.Buffered(1))


def _params(*sem):
    return pltpu.CompilerParams(dimension_semantics=sem)


def _mods_body(c_ref, w_ref, b_ref, o_ref):
    s = _silu(c_ref[...]).astype(BF16)
    o_ref[...] = _dot(s, w_ref[...].astype(BF16)) + b_ref[...]


def _mods(cvec, ada_w, ada_b):
    depth, d, nd = ada_w.shape
    tn = d
    return pl.pallas_call(
        _mods_body,
        grid=(depth, nd // tn),
        in_specs=[
            pl.BlockSpec((MOD_ROWS, d), lambda l, n: (0, 0)),
            pl.BlockSpec((None, d, tn), lambda l, n: (l, 0, n)),
            pl.BlockSpec((None, 1, tn), lambda l, n: (l, 0, n)),
        ],
        out_specs=pl.BlockSpec((None, MOD_ROWS, tn), lambda l, n: (l, 0, n)),
        out_shape=jax.ShapeDtypeStruct((depth, MOD_ROWS, nd), F32),
        compiler_params=_params("parallel", "parallel"),
        name="mods",
    )(cvec, ada_w, ada_b.reshape(depth, 1, nd))


def _ffn_body(*refs, j, has_add, final):
    x_ref, mod_ref, ng_ref, wg_ref, wu_ref, wd_ref = refs[:6]
    rest = list(refs[6:])
    add_ref = rest.pop(0) if has_add else None
    fg_ref = rest.pop(0) if final else None
    o_ref = rest.pop(0)
    x = x_ref[...]
    m = mod_ref[...]
    if has_add:
        x = x + m[5:6] * add_ref[...].astype(F32)
    h = _prenorm_mod(x, ng_ref[j:j + 1], m[3 * j:3 * j + 1], m[3 * j + 1:3 * j + 2]).astype(BF16)
    a = (_silu(_dot(h, wg_ref[...])) * _dot(h, wu_ref[...])).astype(BF16)
    out = x + (0.5 * m[3 * j + 2:3 * j + 3]) * _dot(a, wd_ref[...])
    if final:
        out = _rms(out, fg_ref[...])
    o_ref[...] = out


def _ffn(x, mod, ng, wg, wu, wd, *, j, add=None, final_g=None, tm=512):
    nb, t, d = x.shape
    f = wg.shape[1]
    tok = pl.BlockSpec((None, tm, d), lambda b, i: (b, i, 0))
    in_specs = [tok, pl.BlockSpec((None, N_MOD, d), lambda b, i: (b, 0, 0)),
                _resident((3, d)), _resident((d, f)), _resident((d, f)), _resident((f, d))]
    args = [x, mod, ng, wg, wu, wd]
    if add is not None:
        in_specs.append(tok)
        args.append(add)
    if final_g is not None:
        in_specs.append(_resident((1, d)))
        args.append(final_g.reshape(1, d))
    return pl.pallas_call(
        functools.partial(_ffn_body, j=j, has_add=add is not None, final=final_g is not None),
        grid=(nb, t // tm),
        in_specs=in_specs,
        out_specs=tok,
        out_shape=jax.ShapeDtypeStruct(x.shape, F32),
        compiler_params=_params("parallel", "parallel"),
        name="ffn",
    )(*args)


def _prenorm_body(x_ref, mod_ref, ng_ref, o_ref):
    m = mod_ref[...]
    o_ref[...] = _prenorm_mod(x_ref[...], ng_ref[1:2], m[3:4], m[4:5]).astype(BF16)


def _prenorm(x, mod, ng, tm=1024):
    nb, t, d = x.shape
    tok = pl.BlockSpec((None, tm, d), lambda b, i: (b, i, 0))
    return pl.pallas_call(
        _prenorm_body,
        grid=(nb, t // tm),
        in_specs=[tok, pl.BlockSpec((None, N_MOD, d), lambda b, i: (b, 0, 0)), _resident((3, d))],
        out_specs=tok,
        out_shape=jax.ShapeDtypeStruct(x.shape, BF16),
        compiler_params=_params("parallel", "parallel"),
        name="prenorm",
    )(x, mod, ng)


def _four1_body(ht_ref, f_ref, o_ref, *, group):
    for g in range(group):
        o_ref[g] = _dot(f_ref[g], ht_ref[g]).astype(BF16)


def _four1(ht, ftab, group=4):
    nb, n2, n1, d = ht.shape
    return pl.pallas_call(
        functools.partial(_four1_body, group=group),
        grid=(nb, n2 // group),
        in_specs=[pl.BlockSpec((None, group, n1, d), lambda b, i: (b, i, 0, 0)),
                  pl.BlockSpec((group, 2 * n1, n1), lambda b, i: (i, 0, 0))],
        out_specs=pl.BlockSpec((None, group, 2 * n1, d), lambda b, i: (b, i, 0, 0)),
        out_shape=jax.ShapeDtypeStruct((nb, n2, 2 * n1, d), BF16),
        compiler_params=_params("parallel", "parallel"),
        name="four1",
    )(ht, ftab)


def _channel_mix(xre, xim, cd, sd, w, b):
    gd = cd.shape[0]
    cols = [_dot(xre[:, s:s + gd], cd) + _dot(xim[:, s:s + gd], sd) for s in range(0, xre.shape[1], gd)]
    return _dot(jnp.concatenate(cols, axis=1).astype(BF16), w) + b


def _four2_body(bt_ref, m2_ref, cd_ref, sd_ref, w_ref, b_ref, o_ref, *, group):
    n2 = m2_ref.shape[0] // 2
    res, ims = [], []
    for g in range(group):
        xx = _dot(m2_ref[...], bt_ref[g])
        res.append(xx[:n2])
        ims.append(xx[n2:])
    xre = jnp.concatenate(res, axis=0).astype(BF16)
    xim = jnp.concatenate(ims, axis=0).astype(BF16)
    y = _channel_mix(xre, xim, cd_ref[...], sd_ref[...], w_ref[...], b_ref[...])
    for g in range(group):
        o_ref[g] = y[g * n2:(g + 1) * n2].astype(BF16)


def _four2(bt, m2, cd, sd, w, b, group=4):
    nb, k1, n2x2, d = bt.shape
    gd = cd.shape[0]
    return pl.pallas_call(
        functools.partial(_four2_body, group=group),
        grid=(nb, k1 // group),
        in_specs=[pl.BlockSpec((None, group, n2x2, d), lambda bb, i: (bb, i, 0, 0)),
                  _resident((n2x2, n2x2)), _resident((gd, gd)), _resident((gd, gd)),
                  _resident((d, d)), _resident((1, d))],
        out_specs=pl.BlockSpec((None, group, n2x2 // 2, d), lambda bb, i: (bb, i, 0, 0)),
        out_shape=jax.ShapeDtypeStruct((nb, k1, n2x2 // 2, d), BF16),
        compiler_params=_params("parallel", "parallel"),
        name="four2",
    )(bt, m2, cd, sd, w, b.reshape(1, d))


def _four_ctx_body(x_ref, mod_ref, ng_ref, m_ref, cd_ref, sd_ref, w_ref, b_ref, o_ref):
    x = x_ref[...]
    m = mod_ref[...]
    h = _prenorm_mod(x, ng_ref[1:2], m[3:4], m[4:5]).astype(BF16)
    n = x.shape[0]
    xx = _dot(m_ref[...], h)
    y = _channel_mix(xx[:n].astype(BF16), xx[n:].astype(BF16), cd_ref[...], sd_ref[...], w_ref[...], b_ref[...])
    o_ref[...] = x + m[5:6] * y


def _four_ctx(ctx, mod, ng, mseq, cd, sd, w, b):
    nb, n, d = ctx.shape
    gd = cd.shape[0]
    tok = pl.BlockSpec((None, n, d), lambda bb: (bb, 0, 0))
    return pl.pallas_call(
        _four_ctx_body,
        grid=(nb,),
        in_specs=[tok, pl.BlockSpec((None, N_MOD, d), lambda bb: (0, 0, 0)), _resident((3, d)), _resident((2 * n, n)), _resident((gd, gd)), _resident((gd, gd)),
                  _resident((d, d)), _resident((1, d))],
        out_specs=tok,
        out_shape=jax.ShapeDtypeStruct(ctx.shape, F32),
        compiler_params=_params("parallel"),
        name="four_ctx",
    )(ctx, mod, ng, mseq, cd, sd, w, b.reshape(1, d))


def _dft_tables(n_lat, n_ctx, gd):
    n1 = int(round(n_lat ** 0.5))
    two_pi = 2.0 * np.pi
    k1 = jnp.arange(n1, dtype=jnp.int32)
    pos = (jnp.arange(n1, dtype=jnp.int32)[None, :] * n1 + jnp.arange(n1, dtype=jnp.int32)[:, None])
    r = (k1[None, :, None] * pos[:, None, :]) % n_lat
    ang = r.astype(F32) * (two_pi / n_lat)
    ftab = jnp.concatenate([jnp.cos(ang), -jnp.sin(ang)], axis=1).astype(BF16)

    def cs(n, scale):
        i = jnp.arange(n, dtype=jnp.int32)
        a = ((i[:, None] * i[None, :]) % n).astype(F32) * (two_pi / n)
        return jnp.cos(a) * scale, jnp.sin(a) * scale

    c2, s2 = cs(n1, 1.0 / n1)
    m2 = jnp.concatenate([jnp.concatenate([c2, s2], axis=1),
                          jnp.concatenate([-s2, c2], axis=1)], axis=0).astype(BF16)
    cc, sc = cs(n_ctx, n_ctx ** -0.5)
    mctx = jnp.concatenate([cc, -sc], axis=0).astype(BF16)
    cd, sd = cs(gd, gd ** -0.5)
    return ftab, m2, mctx, cd.astype(BF16), sd.astype(BF16)


def _ret_proj_body(*refs, rope, qk):
    if rope:
        x_ref, mod_ref, ng_ref, w_ref, cos_ref, sin_ref, q_ref, k_ref, v_ref, g_ref = refs
    else:
        x_ref, mod_ref, ng_ref, w_ref, q_ref, k_ref, v_ref, g_ref = refs
    m = mod_ref[...]
    h = _prenorm_mod(x_ref[...], ng_ref[1:2], m[3:4], m[4:5]).astype(BF16)
    dk = qk // HEADS
    half = dk // 2
    for src, dst, scale in ((0, q_ref, 1.0), (qk, k_ref, dk ** -0.5)):
        t = _dot(h, w_ref[:, src:src + qk]) * scale
        for s in range(0, qk, dk):
            t1 = t[:, s:s + half]
            t2 = t[:, s + half:s + dk]
            if rope:
                cos = cos_ref[...]
                sin = sin_ref[...]
                t1, t2 = t1 * cos - t2 * sin, t2 * cos + t1 * sin
            dst[:, s:s + half] = t1.astype(BF16)
            dst[:, s + half:s + dk] = t2.astype(BF16)
    vd = v_ref.shape[-1]
    v_ref[...] = _dot(h, w_ref[:, 2 * qk:2 * qk + vd]).astype(BF16)
    g_ref[...] = _dot(h, w_ref[:, 2 * qk + vd:2 * qk + 2 * vd]).astype(BF16)


def _ret_proj(x, mod, ng, w_in, cos=None, sin=None, *, qk, vd, tm=512):
    nb, t, d = x.shape
    rope = cos is not None
    tok = lambda width: pl.BlockSpec((None, tm, width), lambda b, i: (b, i, 0))
    in_specs = [tok(d), pl.BlockSpec((None, N_MOD, d), lambda b, i: (b, 0, 0)),
                _resident((3, d)), _resident(w_in.shape)]
    args = [x, mod, ng, w_in]
    if rope:
        in_specs += [pl.BlockSpec((tm, cos.shape[1]), lambda b, i: (i, 0))] * 2
        args += [cos, sin]
    return pl.pallas_call(
        functools.partial(_ret_proj_body, rope=rope, qk=qk),
        grid=(nb, t // tm),
        in_specs=in_specs,
        out_specs=[tok(qk), tok(qk), tok(vd), tok(vd)],
        out_shape=[jax.ShapeDtypeStruct((nb, t, w), BF16) for w in (qk, qk, vd, vd)],
        compiler_params=_params("parallel", "parallel"),
        name="ret_proj",
    )(*args)


def _ret_scan_body(*refs, reverse, n_chunks, finish):
    if finish:
        dec_ref, q_ref, k_ref, v_ref, s0_ref, yf_ref, g_ref, y_ref, sfin_ref, s_scr = refs
    else:
        dec_ref, q_ref, k_ref, v_ref, s0_ref, y_ref, sfin_ref, s_scr = refs
    t = pl.program_id(2)

    @pl.when(t == 0)
    def _():
        s_scr[...] = s0_ref[...]

    c = CHUNK
    lg = -jnp.abs(dec_ref[...][:, 0:1])
    row = lax.broadcasted_iota(jnp.int32, (c, c), 0).astype(F32)
    col = lax.broadcasted_iota(jnp.int32, (c, c), 1).astype(F32)
    pos = lax.broadcasted_iota(jnp.int32, (c, 1), 0).astype(F32)
    if reverse:
        diff = col - row
        q_dec = jnp.exp(lg * (c - pos))
        k_dec = jnp.exp(lg * pos)
    else:
        diff = row - col
        q_dec = jnp.exp(lg * (pos + 1.0))
        k_dec = jnp.exp(lg * (c - 1.0 - pos))
    intra = jnp.where(diff >= 0, jnp.exp(lg * jnp.maximum(diff, 0.0)), 0.0)
    chunk_dec = jnp.exp(lg * c)

    order = range(n_chunks - 1, -1, -1) if reverse else range(n_chunks)
    for ci in order:
        sl = pl.ds(ci * c, c)
        q = q_ref[sl, :]
        k = k_ref[sl, :]
        v = v_ref[sl, :]
        s = s_scr[...]
        scores = lax.dot_general(q, k, (((1,), (1,)), ((), ())), preferred_element_type=F32) * intra
        y = _dot(scores.astype(BF16), v) + _dot((q.astype(F32) * q_dec).astype(BF16), s.astype(BF16))
        kd = (k.astype(F32) * k_dec).astype(BF16)
        s_scr[...] = chunk_dec * s + lax.dot_general(kd, v, (((0,), (0,)), ((), ())), preferred_element_type=F32)
        if finish:
            y = y + yf_ref[sl, :]
            mu = jnp.mean(y, axis=-1, keepdims=True)
            yc = y - mu
            yn = yc * lax.rsqrt(jnp.mean(yc * yc, axis=-1, keepdims=True) + EPS)
            y_ref[sl, :] = (_silu(g_ref[sl, :].astype(F32)) * yn).astype(BF16)
        else:
            y_ref[sl, :] = y

    @pl.when(t == pl.num_programs(2) - 1)
    def _():
        sfin_ref[...] = s_scr[...]


def _ret_scan(dec, q, k, v, s0, yf=None, g=None, *, reverse, tb):
    nb, t, qk = q.shape
    vd = v.shape[-1]
    dk, dv = qk // HEADS, vd // HEADS
    nt = t // tb
    finish = yf is not None
    tsel = (lambda i: nt - 1 - i) if reverse else (lambda i: i)
    tokspec = lambda w: pl.BlockSpec((None, tb, w), lambda b, h, i: (b, tsel(i), h))
    state = pl.BlockSpec((None, None, dk, dv), lambda b, h, i: (b, h, 0, 0))
    in_specs = [pl.BlockSpec((None, 1, LANES), lambda b, h, i: (h, 0, 0)),
                tokspec(dk), tokspec(dk), tokspec(dv), state]
    args = [dec, q, k, v, s0]
    if finish:
        in_specs += [tokspec(dv), tokspec(dv)]
        args += [yf, g]
    return pl.pallas_call(
        functools.partial(_ret_scan_body, reverse=reverse, n_chunks=tb // CHUNK, finish=finish),
        grid=(nb, HEADS, nt),
        in_specs=in_specs,
        out_specs=[tokspec(dv), state],
        out_shape=[jax.ShapeDtypeStruct((nb, t, vd), BF16 if finish else F32),
                   jax.ShapeDtypeStruct((nb, HEADS, dk, dv), F32)],
        scratch_shapes=[pltpu.VMEM((dk, dv), F32)],
        compiler_params=_params("parallel", "parallel", "arbitrary"),
        name="ret_scan_bwd" if reverse else "ret_scan_fwd",
    )(*args)


def _ret_out_body(x_ref, mod_ref, a_ref, w_ref, o_ref):
    m = mod_ref[...]
    o_ref[...] = x_ref[...] + m[5:6] * _dot(a_ref[...], w_ref[...])


def _ret_out(x, mod, a, w_out, tm=512):
    nb, t, d = x.shape
    vd = a.shape[-1]
    tok = lambda w: pl.BlockSpec((None, tm, w), lambda b, i: (b, i, 0))
    return pl.pallas_call(
        _ret_out_body,
        grid=(nb, t // tm),
        in_specs=[tok(d), pl.BlockSpec((None, N_MOD, d), lambda b, i: (b, 0, 0)), tok(vd), _resident(w_out.shape)],
        out_specs=tok(d),
        out_shape=jax.ShapeDtypeStruct(x.shape, F32),
        compiler_params=_params("parallel", "parallel"),
        name="ret_out",
    )(x, mod, a, w_out)


def _axial_rope(n_tokens, dk):
    rows = n_tokens // GRID_W
    row = jnp.repeat(jnp.arange(rows, dtype=F32), GRID_W)
    col = jnp.tile(jnp.arange(GRID_W, dtype=F32), rows)
    n_freq = dk // 4
    inv_freq = ROPE_BASE ** (-jnp.arange(n_freq, dtype=F32) / n_freq)
    ang = jnp.concatenate([row[:, None] * inv_freq, col[:, None] * inv_freq], axis=-1)
    return jnp.cos(ang), jnp.sin(ang)


def _retention(x, ctx, m_lat, m_ctx, ng, w_in, w_out, decay, cos, sin, need_ctx_out):
    nb, n, d = x.shape
    n_ctx = ctx.shape[1]
    qk = d
    vd = w_out.shape[0]
    dk, dv = qk // HEADS, vd // HEADS
    dec = jnp.broadcast_to(decay[:, :, None, None], (2, HEADS, 1, LANES))

    ctx_flat = ctx.reshape(1, nb * n_ctx, d)
    qc, kc, vc, gc = [a.reshape(nb, n_ctx, -1) for a in _ret_proj(ctx_flat, m_ctx, ng, w_in, qk=qk, vd=vd)]
    zeros = jnp.zeros((nb, HEADS, dk, dv), F32)
    yc_f, s_f = _ret_scan(dec[0], qc, kc, vc, zeros, reverse=False, tb=n_ctx)
    ac, s_b = _ret_scan(dec[1], qc, kc, vc, zeros, yc_f, gc, reverse=True, tb=n_ctx)

    ql, kl, vl, gl = _ret_proj(x, m_lat, ng, w_in, cos, sin, qk=qk, vd=vd)
    yl_f, _ = _ret_scan(dec[0], ql, kl, vl, s_f, reverse=False, tb=1024)
    al, _ = _ret_scan(dec[1], ql, kl, vl, s_b, yl_f, gl, reverse=True, tb=1024)
    x = _ret_out(x, m_lat, al, w_out)
    if need_ctx_out:
        ctx = _ret_out(ctx_flat, m_ctx, ac.reshape(1, nb * n_ctx, vd), w_out).reshape(ctx.shape)
    return x, ctx


def kernel(x, c, ctx, c_ctx, ada_w, ada_b, norm_g, final_g, ffn_w_gate, ffn_w_up, ffn_w_down,
           four_w, four_b, ret_w_in, ret_w_out, ret_decay):
    nb, n, d = x.shape
    n_ctx = ctx.shape[1]
    depth = ada_w.shape[0]
    n1 = int(round(n ** 0.5))
    assert n1 * n1 == n and nb + 1 <= MOD_ROWS

    cvec = jnp.zeros((MOD_ROWS, d), F32).at[:nb].set(c).at[nb].set(c_ctx)
    mods = _mods(cvec, ada_w, ada_b).reshape(depth, MOD_ROWS, N_MOD, d)
    wg, wu, wd = (w.astype(BF16) for w in (ffn_w_gate, ffn_w_up, ffn_w_down))
    four_wb, w_in, w_out = (w.astype(BF16) for w in (four_w, ret_w_in, ret_w_out))
    ftab, m2, mctx, cd, sd = _dft_tables(n, n_ctx, d // N_GROUPS)
    cos, sin = _axial_rope(n, d // HEADS)

    for l in range(depth):
        last = l == depth - 1
        mixer, slot = l % 2, l // 2
        ctx_read = (not last) or mixer == 1
        m_lat, m_ctx = mods[l, :nb], mods[l, nb:nb + 1]
        ng = norm_g[l]
        flat = lambda a: a.reshape(1, nb * n_ctx, d)

        x = _ffn(x, m_lat, ng, wg[l, 0], wu[l, 0], wd[l, 0], j=0)
        if ctx_read:
            ctx = _ffn(flat(ctx), m_ctx, ng, wg[l, 0], wu[l, 0], wd[l, 0], j=0).reshape(ctx.shape)

        y_add = None
        if mixer == 0:
            h = _prenorm(x, m_lat, ng)
            ht = h.reshape(nb, n1, n1, d).transpose(0, 2, 1, 3)
            bt = _four1(ht, ftab).reshape(nb, n1, 2, n1, d).transpose(0, 3, 2, 1, 4)
            yt = _four2(bt.reshape(nb, n1, 2 * n1, d), m2, cd, sd, four_wb[slot], four_b[slot])
            y_add = yt.transpose(0, 2, 1, 3).reshape(nb, n, d)
            if ctx_read:
                ctx = _four_ctx(ctx, m_ctx, ng, mctx, cd, sd, four_wb[slot], four_b[slot])
        else:
            x, ctx = _retention(x, ctx, m_lat, m_ctx, ng, w_in[slot], w_out[slot], ret_decay[slot],
                                cos, sin, not last)

        x = _ffn(x, m_lat, ng, wg[l, 1], wu[l, 1], wd[l, 1], j=2, add=y_add,
                 final_g=final_g if last else None)
        if not last:
            ctx = _ffn(flat(ctx), m_ctx, ng, wg[l, 1], wu[l, 1], wd[l, 1], j=2).reshape(ctx.shape)
    return x
```

```python
import functools
from typing import NamedTuple

import jax
import jax.numpy as jnp
import numpy as np
from jax import lax
from jax.experimental import pallas as pl
from jax.experimental.pallas import tpu as pltpu

F32 = jnp.float32
BF16 = jnp.bfloat16

GRID_W = 64
N_GROUPS = 4
HEADS = 4
RET_CHUNK = 256
RET_BLOCK = 512
ROPE_BASE = 10000.0
N_MOD = 9
EPS = 1e-6
LANES = 128
MOD_ROWS = 8


class Layer(NamedTuple):
    mods: jax.Array
    norm_g: jax.Array
    index: int
    row: int | None


def _dot(a, b):
    return jnp.dot(a, b, preferred_element_type=F32)


def _rms(x, g):
    return x * lax.rsqrt(jnp.mean(x * x, axis=-1, keepdims=True) + EPS) * g


def _prenorm_mod(x, g, shift, scale):
    return _rms(x, g) * (1.0 + scale) + shift


def _silu(x):
    return x * jax.nn.sigmoid(x)


def _whole(arr, *idx):
    shape = (None,) * len(idx) + tuple(arr.shape[len(idx):])
    tail = (0,) * (arr.ndim - len(idx))
    return pl.BlockSpec(shape, lambda *_: idx + tail, pipeline_mode=pl.Buffered(1)), arr


def _layer_blocks(lay):
    d = lay.mods.shape[-1]
    l, row = lay.index, lay.row
    pick = (lambda b, *_: (l, b, 0, 0)) if row is None else (lambda *_: (l, row, 0, 0))
    return [(pl.BlockSpec((None, None, N_MOD, d), pick), lay.mods), _whole(lay.norm_g, l)]


def _call(body, operands, **kw):
    specs, args = zip(*operands)
    return pl.pallas_call(body, in_specs=list(specs), **kw)(*args)


def _params(*sem):
    return pltpu.CompilerParams(dimension_semantics=sem)


def _mods_body(c_ref, w_ref, b_ref, o_ref):
    s = _silu(c_ref[...]).astype(BF16)
    o_ref[...] = _dot(s, w_ref[...].astype(BF16)) + b_ref[...]


def _mods(cvec, ada_w, ada_b):
    depth, d, nd = ada_w.shape
    tn = d
    return pl.pallas_call(
        _mods_body,
        grid=(depth, nd // tn),
        in_specs=[
            pl.BlockSpec((MOD_ROWS, d), lambda l, n: (0, 0)),
            pl.BlockSpec((None, d, tn), lambda l, n: (l, 0, n)),
            pl.BlockSpec((None, 1, tn), lambda l, n: (l, 0, n)),
        ],
        out_specs=pl.BlockSpec((None, MOD_ROWS, tn), lambda l, n: (l, 0, n)),
        out_shape=jax.ShapeDtypeStruct((depth, MOD_ROWS, nd), F32),
        compiler_params=_params("parallel", "parallel"),
        name="mods",
    )(cvec, ada_w, ada_b.reshape(depth, 1, nd))


def _ffn_body(*refs, j, has_add, final, emit_h):
    x_ref, mod_ref, ng_ref, wg_ref, wu_ref, wd_ref = refs[:6]
    rest = list(refs[6:])
    add_ref = rest.pop(0) if has_add else None
    fg_ref = rest.pop(0) if final else None
    o_ref = rest.pop(0)
    x = x_ref[...]
    m = mod_ref[...]
    ng = ng_ref[...]
    if has_add:
        x = x + m[5:6] * add_ref[...].astype(F32)
    h = _prenorm_mod(x, ng[j:j + 1], m[3 * j:3 * j + 1], m[3 * j + 1:3 * j + 2]).astype(BF16)
    a = (_silu(_dot(h, wg_ref[...])) * _dot(h, wu_ref[...])).astype(BF16)
    out = x + (0.5 * m[3 * j + 2:3 * j + 3]) * _dot(a, wd_ref[...])
    if emit_h:
        rest.pop(0)[...] = _prenorm_mod(out, ng[1:2], m[3:4], m[4:5]).astype(BF16)
    if final:
        out = _rms(out, fg_ref[...])
    o_ref[...] = out


def _ffn(x, lay, w3, jj, *, j, add=None, final_g=None, emit_h=False, tm=512):
    nb, t, d = x.shape
    tok = pl.BlockSpec((None, tm, d), lambda b, i: (b, i, 0))
    operands = [(tok, x)] + _layer_blocks(lay) + [_whole(w, lay.index, jj) for w in w3]
    if add is not None:
        operands.append((tok, add))
    if final_g is not None:
        operands.append(_whole(final_g.reshape(1, d)))
    out_shape = [jax.ShapeDtypeStruct(x.shape, F32)] + ([jax.ShapeDtypeStruct(x.shape, BF16)] if emit_h else [])
    out = _call(
        functools.partial(_ffn_body, j=j, has_add=add is not None, final=final_g is not None, emit_h=emit_h),
        operands,
        grid=(nb, t // tm),
        out_specs=[tok] * len(out_shape),
        out_shape=out_shape,
        compiler_params=_params("parallel", "parallel"),
        name="ffn",
    )
    return out if emit_h else out[0]


def _four1_body(ht_ref, f_ref, o_ref, *, group):
    for g in range(group):
        o_ref[g] = _dot(f_ref[g], ht_ref[g]).astype(BF16)


def _four1(ht, ftab, group=4):
    nb, n2, n1, d = ht.shape
    return pl.pallas_call(
        functools.partial(_four1_body, group=group),
        grid=(nb, n2 // group),
        in_specs=[pl.BlockSpec((None, group, n1, d), lambda b, i: (b, i, 0, 0)),
                  pl.BlockSpec((group, 2 * n1, n1), lambda b, i: (i, 0, 0))],
        out_specs=pl.BlockSpec((None, group, 2 * n1, d), lambda b, i: (b, i, 0, 0)),
        out_shape=jax.ShapeDtypeStruct((nb, n2, 2 * n1, d), BF16),
        compiler_params=_params("parallel", "parallel"),
        name="four1",
    )(ht, ftab)


def _channel_mix(xre, xim, cd, sd, w, b):
    gd = cd.shape[0]
    cols = [_dot(xre[:, s:s + gd], cd) + _dot(xim[:, s:s + gd], sd) for s in range(0, xre.shape[1], gd)]
    return _dot(jnp.concatenate(cols, axis=1).astype(BF16), w) + b


def _four2_body(bt_ref, m2_ref, cd_ref, sd_ref, w_ref, b_ref, o_ref, *, group):
    n2 = m2_ref.shape[0] // 2
    res, ims = [], []
    for g in range(group):
        xx = _dot(m2_ref[...], bt_ref[g])
        res.append(xx[:n2])
        ims.append(xx[n2:])
    xre = jnp.concatenate(res, axis=0).astype(BF16)
    xim = jnp.concatenate(ims, axis=0).astype(BF16)
    y = _channel_mix(xre, xim, cd_ref[...], sd_ref[...], w_ref[...], b_ref[...])
    for g in range(group):
        o_ref[g] = y[g * n2:(g + 1) * n2].astype(BF16)


def _four2(bt, m2, cd, sd, w, b, slot, group=4):
    nb, k1, n2x2, d = bt.shape
    operands = [(pl.BlockSpec((None, group, n2x2, d), lambda bb, i: (bb, i, 0, 0)), bt),
                _whole(m2), _whole(cd), _whole(sd), _whole(w, slot), _whole(b, slot)]
    return _call(
        functools.partial(_four2_body, group=group),
        operands,
        grid=(nb, k1 // group),
        out_specs=pl.BlockSpec((None, group, n2x2 // 2, d), lambda bb, i: (bb, i, 0, 0)),
        out_shape=jax.ShapeDtypeStruct((nb, k1, n2x2 // 2, d), BF16),
        compiler_params=_params("parallel", "parallel"),
        name="four2",
    )


def _four_ctx_body(x_ref, mod_ref, ng_ref, m_ref, cd_ref, sd_ref, w_ref, b_ref, o_ref):
    x = x_ref[...]
    m = mod_ref[...]
    h = _prenorm_mod(x, ng_ref[1:2], m[3:4], m[4:5]).astype(BF16)
    n = x.shape[0]
    xx = _dot(m_ref[...], h)
    y = _channel_mix(xx[:n].astype(BF16), xx[n:].astype(BF16), cd_ref[...], sd_ref[...], w_ref[...], b_ref[...])
    o_ref[...] = x + m[5:6] * y


def _four_ctx(ctx, lay, mseq, cd, sd, w, b, slot):
    nb, n, d = ctx.shape
    tok = pl.BlockSpec((None, n, d), lambda bb: (bb, 0, 0))
    operands = [(tok, ctx)] + _layer_blocks(lay) + [_whole(mseq), _whole(cd), _whole(sd),
                                                    _whole(w, slot), _whole(b, slot)]
    return _call(
        _four_ctx_body,
        operands,
        grid=(nb,),
        out_specs=tok,
        out_shape=jax.ShapeDtypeStruct(ctx.shape, F32),
        compiler_params=_params("parallel"),
        name="four_ctx",
    )


def _dft_tables(n_lat, n_ctx, gd):
    n1 = int(round(n_lat ** 0.5))
    two_pi = 2.0 * np.pi
    k1 = jnp.arange(n1, dtype=jnp.int32)
    pos = (jnp.arange(n1, dtype=jnp.int32)[None, :] * n1 + jnp.arange(n1, dtype=jnp.int32)[:, None])
    r = (k1[None, :, None] * pos[:, None, :]) % n_lat
    ang = r.astype(F32) * (two_pi / n_lat)
    ftab = jnp.concatenate([jnp.cos(ang), -jnp.sin(ang)], axis=1).astype(BF16)

    def cs(n, scale):
        i = jnp.arange(n, dtype=jnp.int32)
        a = ((i[:, None] * i[None, :]) % n).astype(F32) * (two_pi / n)
        return jnp.cos(a) * scale, jnp.sin(a) * scale

    c2, s2 = cs(n1, 1.0 / n1)
    m2 = jnp.concatenate([jnp.concatenate([c2, s2], axis=1),
                          jnp.concatenate([-s2, c2], axis=1)], axis=0).astype(BF16)
    cc, sc = cs(n_ctx, n_ctx ** -0.5)
    mctx = jnp.concatenate([cc, -sc], axis=0).astype(BF16)
    cd, sd = cs(gd, gd ** -0.5)
    return ftab, m2, mctx, cd.astype(BF16), sd.astype(BF16)


def _ret_proj_body(*refs, rope, qk):
    if rope:
        x_ref, mod_ref, ng_ref, w_ref, cos_ref, sin_ref, q_ref, k_ref, v_ref, g_ref = refs
    else:
        x_ref, mod_ref, ng_ref, w_ref, q_ref, k_ref, v_ref, g_ref = refs
    m = mod_ref[...]
    h = _prenorm_mod(x_ref[...], ng_ref[1:2], m[3:4], m[4:5]).astype(BF16)
    dk = qk // HEADS
    half = dk // 2
    for src, dst, scale in ((0, q_ref, 1.0), (qk, k_ref, dk ** -0.5)):
        t = _dot(h, w_ref[:, src:src + qk]) * scale
        for s in range(0, qk, dk):
            t1 = t[:, s:s + half]
            t2 = t[:, s + half:s + dk]
            if rope:
                cos = cos_ref[...]
                sin = sin_ref[...]
                t1, t2 = t1 * cos - t2 * sin, t2 * cos + t1 * sin
            dst[:, s:s + half] = t1.astype(BF16)
            dst[:, s + half:s + dk] = t2.astype(BF16)
    vd = v_ref.shape[-1]
    v_ref[...] = _dot(h, w_ref[:, 2 * qk:2 * qk + vd]).astype(BF16)
    g_ref[...] = _dot(h, w_ref[:, 2 * qk + vd:2 * qk + 2 * vd]).astype(BF16)


def _ret_proj(x, lay, w_in, slot, cos=None, sin=None, *, qk, vd, tm=512):
    nb, t, d = x.shape
    rope = cos is not None
    tok = lambda width: pl.BlockSpec((None, tm, width), lambda b, i: (b, i, 0))
    operands = [(tok(d), x)] + _layer_blocks(lay) + [_whole(w_in, slot)]
    if rope:
        pos = pl.BlockSpec((tm, cos.shape[1]), lambda b, i: (i, 0))
        operands += [(pos, cos), (pos, sin)]
    return _call(
        functools.partial(_ret_proj_body, rope=rope, qk=qk),
        operands,
        grid=(nb, t // tm),
        out_specs=[tok(qk), tok(qk), tok(vd), tok(vd)],
        out_shape=[jax.ShapeDtypeStruct((nb, t, w), BF16) for w in (qk, qk, vd, vd)],
        compiler_params=_params("parallel", "parallel"),
        name="ret_proj",
    )


def _ret_scan_body(*refs, reverse, chunk, finish):
    if finish:
        dec_ref, q_ref, k_ref, v_ref, s0_ref, yf_ref, g_ref, y_ref, sfin_ref, s_scr = refs
    else:
        dec_ref, q_ref, k_ref, v_ref, s0_ref, y_ref, sfin_ref, s_scr = refs
    t = pl.program_id(1)

    @pl.when(t == 0)
    def _():
        s_scr[...] = s0_ref[...]

    c = chunk
    n_chunks = q_ref.shape[0] // c
    dk, dv = s_scr.shape[1:]
    row = lax.broadcasted_iota(jnp.int32, (c, c), 0).astype(F32)
    col = lax.broadcasted_iota(jnp.int32, (c, c), 1).astype(F32)
    pos = lax.broadcasted_iota(jnp.int32, (c, 1), 0).astype(F32)
    diff = col - row if reverse else row - col
    decays = []
    for hd in range(HEADS):
        lg = -jnp.abs(dec_ref[hd][:, 0:1])
        if reverse:
            q_dec, k_dec = jnp.exp(lg * (c - pos)), jnp.exp(lg * pos)
        else:
            q_dec, k_dec = jnp.exp(lg * (pos + 1.0)), jnp.exp(lg * (c - 1.0 - pos))
        intra = jnp.where(diff >= 0, jnp.exp(lg * jnp.maximum(diff, 0.0)), 0.0)
        decays.append((q_dec, k_dec, intra, jnp.exp(lg * c)))

    for ci in (range(n_chunks - 1, -1, -1) if reverse else range(n_chunks)):
        sl = pl.ds(ci * c, c)
        for hd in range(HEADS):
            q_dec, k_dec, intra, chunk_dec = decays[hd]
            qs, vs = pl.ds(hd * dk, dk), pl.ds(hd * dv, dv)
            q = q_ref[sl, qs]
            k = k_ref[sl, qs]
            v = v_ref[sl, vs]
            s = s_scr[hd]
            scores = lax.dot_general(q, k, (((1,), (1,)), ((), ())), preferred_element_type=F32) * intra
            y = _dot(scores.astype(BF16), v) + _dot((q.astype(F32) * q_dec).astype(BF16), s.astype(BF16))
            kd = (k.astype(F32) * k_dec).astype(BF16)
            s_scr[hd] = chunk_dec * s + lax.dot_general(kd, v, (((0,), (0,)), ((), ())),
                                                        preferred_element_type=F32)
            if finish:
                y = y + yf_ref[sl, vs]
                mu = jnp.mean(y, axis=-1, keepdims=True)
                yc = y - mu
                yn = yc * lax.rsqrt(jnp.mean(yc * yc, axis=-1, keepdims=True) + EPS)
                y_ref[sl, vs] = (_silu(g_ref[sl, vs].astype(F32)) * yn).astype(BF16)
            else:
                y_ref[sl, vs] = y

    @pl.when(t == pl.num_programs(1) - 1)
    def _():
        sfin_ref[...] = s_scr[...]


def _ret_scan(dec, q, k, v, s0, yf=None, g=None, *, reverse, tb, chunk):
    nb, t, qk = q.shape
    vd = v.shape[-1]
    dk, dv = qk // HEADS, vd // HEADS
    nt = t // tb
    finish = yf is not None
    tsel = (lambda i: nt - 1 - i) if reverse else (lambda i: i)
    tokspec = lambda w: pl.BlockSpec((None, tb, w), lambda b, i: (b, tsel(i), 0))
    state = pl.BlockSpec((None, HEADS, dk, dv), lambda b, i: (b, 0, 0, 0))
    operands = [_whole(dec), (tokspec(qk), q), (tokspec(qk), k), (tokspec(vd), v), (state, s0)]
    if finish:
        operands += [(tokspec(vd), yf), (tokspec(vd), g)]
    return _call(
        functools.partial(_ret_scan_body, reverse=reverse, chunk=chunk, finish=finish),
        operands,
        grid=(nb, nt),
        out_specs=[tokspec(vd), state],
        out_shape=[jax.ShapeDtypeStruct((nb, t, vd), BF16 if finish else F32),
                   jax.ShapeDtypeStruct((nb, HEADS, dk, dv), F32)],
        scratch_shapes=[pltpu.VMEM((HEADS, dk, dv), F32)],
        compiler_params=_params("parallel", "arbitrary"),
        name="ret_scan_bwd" if reverse else "ret_scan_fwd",
    )


def _ret_out_body(x_ref, mod_ref, ng_ref, a_ref, w_ref, o_ref):
    del ng_ref
    m = mod_ref[...]
    o_ref[...] = x_ref[...] + m[5:6] * _dot(a_ref[...], w_ref[...])


def _ret_out(x, lay, a, w_out, slot, tm=512):
    nb, t, d = x.shape
    vd = a.shape[-1]
    tok = lambda w: pl.BlockSpec((None, tm, w), lambda b, i: (b, i, 0))
    operands = [(tok(d), x)] + _layer_blocks(lay) + [(tok(vd), a), _whole(w_out, slot)]
    return _call(
        _ret_out_body,
        operands,
        grid=(nb, t // tm),
        out_specs=tok(d),
        out_shape=jax.ShapeDtypeStruct(x.shape, F32),
        compiler_params=_params("parallel", "parallel"),
        name="ret_out",
    )


def _axial_rope(n_tokens, dk):
    rows = n_tokens // GRID_W
    row = jnp.repeat(jnp.arange(rows, dtype=F32), GRID_W)
    col = jnp.tile(jnp.arange(GRID_W, dtype=F32), rows)
    n_freq = dk // 4
    inv_freq = ROPE_BASE ** (-jnp.arange(n_freq, dtype=F32) / n_freq)
    ang = jnp.concatenate([row[:, None] * inv_freq, col[:, None] * inv_freq], axis=-1)
    return jnp.cos(ang), jnp.sin(ang)


def _retention(x, ctx, lat, cx, w_in, w_out, slot, decay, cos, sin, need_ctx_out):
    nb, n, d = x.shape
    n_ctx = ctx.shape[1]
    qk = d
    vd = w_out.shape[1]
    dk, dv = qk // HEADS, vd // HEADS
    dec = jnp.broadcast_to(decay[:, :, None, None], (2, HEADS, 1, LANES))

    ctx_flat = ctx.reshape(1, nb * n_ctx, d)
    qc, kc, vc, gc = [a.reshape(nb, n_ctx, -1) for a in _ret_proj(ctx_flat, cx, w_in, slot, qk=qk, vd=vd)]
    zeros = jnp.zeros((nb, HEADS, dk, dv), F32)
    yc_f, s_f = _ret_scan(dec[0], qc, kc, vc, zeros, reverse=False, tb=n_ctx, chunk=RET_CHUNK)
    ac, s_b = _ret_scan(dec[1], qc, kc, vc, zeros, yc_f, gc, reverse=True, tb=n_ctx, chunk=RET_CHUNK)

    ql, kl, vl, gl = _ret_proj(x, lat, w_in, slot, cos, sin, qk=qk, vd=vd)
    yl_f, _ = _ret_scan(dec[0], ql, kl, vl, s_f, reverse=False, tb=RET_BLOCK, chunk=RET_CHUNK)
    al, _ = _ret_scan(dec[1], ql, kl, vl, s_b, yl_f, gl, reverse=True, tb=RET_BLOCK, chunk=RET_CHUNK)
    x = _ret_out(x, lat, al, w_out, slot)
    if need_ctx_out:
        ctx = _ret_out(ctx_flat, cx, ac.reshape(1, nb * n_ctx, vd), w_out, slot).reshape(ctx.shape)
    return x, ctx


def kernel(x, c, ctx, c_ctx, ada_w, ada_b, norm_g, final_g, ffn_w_gate, ffn_w_up, ffn_w_down,
           four_w, four_b, ret_w_in, ret_w_out, ret_decay):
    nb, n, d = x.shape
    n_ctx = ctx.shape[1]
    depth = ada_w.shape[0]
    n1 = int(round(n ** 0.5))
    assert n1 * n1 == n and nb + 1 <= MOD_ROWS

    cvec = jnp.zeros((MOD_ROWS, d), F32).at[:nb].set(c).at[nb].set(c_ctx)
    mods = _mods(cvec, ada_w, ada_b).reshape(depth, MOD_ROWS, N_MOD, d)
    w3 = tuple(w.astype(BF16) for w in (ffn_w_gate, ffn_w_up, ffn_w_down))
    four_wb, w_in, w_out = (w.astype(BF16) for w in (four_w, ret_w_in, ret_w_out))
    four_b3 = four_b.reshape(four_b.shape[0], 1, d)
    ftab, m2, mctx, cd, sd = _dft_tables(n, n_ctx, d // N_GROUPS)
    cos, sin = _axial_rope(n, d // HEADS)
    flat = lambda a: a.reshape(1, nb * n_ctx, d)

    for l in range(depth):
        last = l == depth - 1
        mixer, slot = l % 2, l // 2
        ctx_read = (not last) or mixer == 1
        lat = Layer(mods, norm_g, l, None)
        cx = Layer(mods, norm_g, l, nb)

        if mixer == 0:
            x, h = _ffn(x, lat, w3, 0, j=0, emit_h=True)
        else:
            x = _ffn(x, lat, w3, 0, j=0)
        if ctx_read:
            ctx = _ffn(flat(ctx), cx, w3, 0, j=0).reshape(ctx.shape)

        y_add = None
        if mixer == 0:
            ht = h.reshape(nb, n1, n1, d).transpose(0, 2, 1, 3)
            bt = _four1(ht, ftab).reshape(nb, n1, 2, n1, d).transpose(0, 3, 2, 1, 4)
            yt = _four2(bt.reshape(nb, n1, 2 * n1, d), m2, cd, sd, four_wb, four_b3, slot)
            y_add = yt.transpose(0, 2, 1, 3).reshape(nb, n, d)
            if ctx_read:
                ctx = _four_ctx(ctx, cx, mctx, cd, sd, four_wb, four_b3, slot)
        else:
            x, ctx = _retention(x, ctx, lat, cx, w_in, w_out, slot, ret_decay[slot], cos, sin, not last)

        x = _ffn(x, lat, w3, 1, j=2, add=y_add, final_g=final_g if last else None)
        if not last:
            ctx = _ffn(flat(ctx), cx, w3, 1, j=2).reshape(ctx.shape)
    return x
```

```python
import functools
from typing import NamedTuple

import jax
import jax.numpy as jnp
import numpy as np
from jax import lax
from jax.experimental import pallas as pl
from jax.experimental.pallas import tpu as pltpu

F32 = jnp.float32
BF16 = jnp.bfloat16

GRID_W = 64
N_GROUPS = 4
HEADS = 4
RET_CHUNK = 256
RET_BLOCK_FWD = 1024
RET_BLOCK_BWD = 512
FOUR_GROUP = 16
FFN_TILE = 1024
ROPE_BASE = 10000.0
N_MOD = 9
EPS = 1e-6
LANES = 128
MOD_ROWS = 8


class Layer(NamedTuple):
    mods: jax.Array
    norm_g: jax.Array
    index: int
    row: int | None


def _dot(a, b):
    return jnp.dot(a, b, preferred_element_type=F32)


def _rms(x, g):
    return x * lax.rsqrt(jnp.mean(x * x, axis=-1, keepdims=True) + EPS) * g


def _prenorm_mod(x, g, shift, scale):
    return _rms(x, g) * (1.0 + scale) + shift


def _silu(x):
    return x * jax.nn.sigmoid(x)


def _whole(arr, *idx):
    shape = (None,) * len(idx) + tuple(arr.shape[len(idx):])
    tail = (0,) * (arr.ndim - len(idx))
    return pl.BlockSpec(shape, lambda *_: idx + tail, pipeline_mode=pl.Buffered(1)), arr


def _layer_blocks(lay):
    d = lay.mods.shape[-1]
    l, row = lay.index, lay.row
    pick = (lambda b, *_: (l, b, 0, 0)) if row is None else (lambda *_: (l, row, 0, 0))
    return [(pl.BlockSpec((None, None, N_MOD, d), pick), lay.mods), _whole(lay.norm_g, l)]


def _call(body, operands, **kw):
    specs, args = zip(*operands)
    return pl.pallas_call(body, in_specs=list(specs), **kw)(*args)


def _params(*sem):
    return pltpu.CompilerParams(dimension_semantics=sem)


def _mods_body(c_ref, w_ref, b_ref, o_ref):
    s = _silu(c_ref[...]).astype(BF16)
    o_ref[...] = _dot(s, w_ref[...].astype(BF16)) + b_ref[...]


def _mods(cvec, ada_w, ada_b):
    depth, d, nd = ada_w.shape
    tn = d
    return pl.pallas_call(
        _mods_body,
        grid=(depth, nd // tn),
        in_specs=[
            pl.BlockSpec((MOD_ROWS, d), lambda l, n: (0, 0)),
            pl.BlockSpec((None, d, tn), lambda l, n: (l, 0, n)),
            pl.BlockSpec((None, 1, tn), lambda l, n: (l, 0, n)),
        ],
        out_specs=pl.BlockSpec((None, MOD_ROWS, tn), lambda l, n: (l, 0, n)),
        out_shape=jax.ShapeDtypeStruct((depth, MOD_ROWS, nd), F32),
        compiler_params=_params("parallel", "parallel"),
        name="mods",
    )(cvec, ada_w, ada_b.reshape(depth, 1, nd))


def _ffn_body(*refs, j, has_add, final, emit_h):
    x_ref, mod_ref, ng_ref, wg_ref, wu_ref, wd_ref = refs[:6]
    rest = list(refs[6:])
    add_ref = rest.pop(0) if has_add else None
    fg_ref = rest.pop(0) if final else None
    o_ref = rest.pop(0)
    x = x_ref[...]
    m = mod_ref[...]
    ng = ng_ref[...]
    if has_add:
        x = x + m[5:6] * add_ref[...].astype(F32)
    h = _prenorm_mod(x, ng[j:j + 1], m[3 * j:3 * j + 1], m[3 * j + 1:3 * j + 2]).astype(BF16)
    a = (_silu(_dot(h, wg_ref[...])) * _dot(h, wu_ref[...])).astype(BF16)
    out = x + (0.5 * m[3 * j + 2:3 * j + 3]) * _dot(a, wd_ref[...])
    if emit_h:
        rest.pop(0)[...] = _prenorm_mod(out, ng[1:2], m[3:4], m[4:5]).astype(BF16)
    if final:
        out = _rms(out, fg_ref[...])
    o_ref[...] = out


def _ffn(x, lay, w3, jj, *, j, add=None, final_g=None, emit_h=False):
    nb, t, d = x.shape
    tm = min(t, FFN_TILE)
    tok = pl.BlockSpec((None, tm, d), lambda b, i: (b, i, 0))
    operands = [(tok, x)] + _layer_blocks(lay) + [_whole(w, lay.index, jj) for w in w3]
    if add is not None:
        operands.append((tok, add))
    if final_g is not None:
        operands.append(_whole(final_g.reshape(1, d)))
    out_shape = [jax.ShapeDtypeStruct(x.shape, F32)] + ([jax.ShapeDtypeStruct(x.shape, BF16)] if emit_h else [])
    out = _call(
        functools.partial(_ffn_body, j=j, has_add=add is not None, final=final_g is not None, emit_h=emit_h),
        operands,
        grid=(nb, t // tm),
        out_specs=[tok] * len(out_shape),
        out_shape=out_shape,
        compiler_params=_params("parallel", "parallel"),
        name="ffn",
    )
    return out if emit_h else out[0]


def _four1_body(h_ref, f_ref, o_ref):
    hs = jnp.swapaxes(h_ref[...], 0, 1)
    for g in range(hs.shape[0]):
        o_ref[g] = _dot(f_ref[g], hs[g]).astype(BF16)


def _four1(h4, ftab, group=FOUR_GROUP):
    nb, n1, n2, d = h4.shape
    return pl.pallas_call(
        _four1_body,
        grid=(nb, n2 // group),
        in_specs=[pl.BlockSpec((None, n1, group, d), lambda b, i: (b, 0, i, 0)),
                  pl.BlockSpec((group, 2 * n1, n1), lambda b, i: (i, 0, 0))],
        out_specs=pl.BlockSpec((None, group, 2 * n1, d), lambda b, i: (b, i, 0, 0)),
        out_shape=jax.ShapeDtypeStruct((nb, n2, 2 * n1, d), BF16),
        compiler_params=_params("parallel", "parallel"),
        name="four1",
    )(h4, ftab)


def _channel_mix(xre, xim, cd, sd, w, b):
    gd = cd.shape[0]
    cols = [_dot(xre[:, s:s + gd], cd) + _dot(xim[:, s:s + gd], sd) for s in range(0, xre.shape[1], gd)]
    return _dot(jnp.concatenate(cols, axis=1).astype(BF16), w) + b


def _four2_body(re_ref, im_ref, m2_ref, cd_ref, sd_ref, w_ref, b_ref, o_ref, *, sub):
    n2, group, d = re_ref.shape
    bre = jnp.swapaxes(re_ref[...], 0, 1)
    bim = jnp.swapaxes(im_ref[...], 0, 1)
    ys = []
    for g0 in range(0, group, sub):
        res, ims = [], []
        for g in range(g0, g0 + sub):
            xx = _dot(m2_ref[...], jnp.concatenate([bre[g], bim[g]], axis=0))
            res.append(xx[:n2])
            ims.append(xx[n2:])
        xre = jnp.concatenate(res, axis=0).astype(BF16)
        xim = jnp.concatenate(ims, axis=0).astype(BF16)
        ys.append(_channel_mix(xre, xim, cd_ref[...], sd_ref[...], w_ref[...], b_ref[...]).astype(BF16))
    y = jnp.concatenate(ys, axis=0).reshape(group, n2, d)
    o_ref[...] = jnp.swapaxes(y, 0, 1)


def _four2(b1, m2, cd, sd, w, b, slot, group=FOUR_GROUP, sub=4):
    nb, n2, k1x2, d = b1.shape
    k1 = k1x2 // 2
    part = lambda off: pl.BlockSpec((None, n2, group, d), lambda bb, i: (bb, 0, off + i, 0))
    operands = [(part(0), b1), (part(k1 // group), b1),
                _whole(m2), _whole(cd), _whole(sd), _whole(w, slot), _whole(b, slot)]
    return _call(
        functools.partial(_four2_body, sub=sub),
        operands,
        grid=(nb, k1 // group),
        out_specs=pl.BlockSpec((None, n2, group, d), lambda bb, i: (bb, 0, i, 0)),
        out_shape=jax.ShapeDtypeStruct((nb, n2, k1, d), BF16),
        compiler_params=_params("parallel", "parallel"),
        name="four2",
    )


def _four_ctx_body(x_ref, mod_ref, ng_ref, m_ref, cd_ref, sd_ref, w_ref, b_ref, o_ref):
    x = x_ref[...]
    m = mod_ref[...]
    h = _prenorm_mod(x, ng_ref[1:2], m[3:4], m[4:5]).astype(BF16)
    n = x.shape[0]
    xx = _dot(m_ref[...], h)
    y = _channel_mix(xx[:n].astype(BF16), xx[n:].astype(BF16), cd_ref[...], sd_ref[...], w_ref[...], b_ref[...])
    o_ref[...] = x + m[5:6] * y


def _four_ctx(ctx, lay, mseq, cd, sd, w, b, slot):
    nb, n, d = ctx.shape
    tok = pl.BlockSpec((None, n, d), lambda bb: (bb, 0, 0))
    operands = [(tok, ctx)] + _layer_blocks(lay) + [_whole(mseq), _whole(cd), _whole(sd),
                                                    _whole(w, slot), _whole(b, slot)]
    return _call(
        _four_ctx_body,
        operands,
        grid=(nb,),
        out_specs=tok,
        out_shape=jax.ShapeDtypeStruct(ctx.shape, F32),
        compiler_params=_params("parallel"),
        name="four_ctx",
    )


def _dft_tables(n_lat, n_ctx, gd):
    n1 = int(round(n_lat ** 0.5))
    two_pi = 2.0 * np.pi
    k1 = jnp.arange(n1, dtype=jnp.int32)
    pos = (jnp.arange(n1, dtype=jnp.int32)[None, :] * n1 + jnp.arange(n1, dtype=jnp.int32)[:, None])
    r = (k1[None, :, None] * pos[:, None, :]) % n_lat
    ang = r.astype(F32) * (two_pi / n_lat)
    ftab = jnp.concatenate([jnp.cos(ang), -jnp.sin(ang)], axis=1).astype(BF16)

    def cs(n, scale):
        i = jnp.arange(n, dtype=jnp.int32)
        a = ((i[:, None] * i[None, :]) % n).astype(F32) * (two_pi / n)
        return jnp.cos(a) * scale, jnp.sin(a) * scale

    c2, s2 = cs(n1, 1.0 / n1)
    m2 = jnp.concatenate([jnp.concatenate([c2, s2], axis=1),
                          jnp.concatenate([-s2, c2], axis=1)], axis=0).astype(BF16)
    cc, sc = cs(n_ctx, n_ctx ** -0.5)
    mctx = jnp.concatenate([cc, -sc], axis=0).astype(BF16)
    cd, sd = cs(gd, gd ** -0.5)
    return ftab, m2, mctx, cd.astype(BF16), sd.astype(BF16)


def _ret_proj_body(*refs, rope, qk):
    if rope:
        x_ref, mod_ref, ng_ref, w_ref, cos_ref, sin_ref, q_ref, k_ref, v_ref, g_ref = refs
    else:
        x_ref, mod_ref, ng_ref, w_ref, q_ref, k_ref, v_ref, g_ref = refs
    m = mod_ref[...]
    h = _prenorm_mod(x_ref[...], ng_ref[1:2], m[3:4], m[4:5]).astype(BF16)
    dk = qk // HEADS
    half = dk // 2
    for src, dst, scale in ((0, q_ref, 1.0), (qk, k_ref, dk ** -0.5)):
        t = _dot(h, w_ref[:, src:src + qk]) * scale
        for s in range(0, qk, dk):
            t1 = t[:, s:s + half]
            t2 = t[:, s + half:s + dk]
            if rope:
                cos = cos_ref[...]
                sin = sin_ref[...]
                t1, t2 = t1 * cos - t2 * sin, t2 * cos + t1 * sin
            dst[:, s:s + half] = t1.astype(BF16)
            dst[:, s + half:s + dk] = t2.astype(BF16)
    vd = v_ref.shape[-1]
    v_ref[...] = _dot(h, w_ref[:, 2 * qk:2 * qk + vd]).astype(BF16)
    g_ref[...] = _dot(h, w_ref[:, 2 * qk + vd:2 * qk + 2 * vd]).astype(BF16)


def _ret_proj(x, lay, w_in, slot, cos=None, sin=None, *, qk, vd, tm=512):
    nb, t, d = x.shape
    rope = cos is not None
    tok = lambda width: pl.BlockSpec((None, tm, width), lambda b, i: (b, i, 0))
    operands = [(tok(d), x)] + _layer_blocks(lay) + [_whole(w_in, slot)]
    if rope:
        pos = pl.BlockSpec((tm, cos.shape[1]), lambda b, i: (i, 0))
        operands += [(pos, cos), (pos, sin)]
    return _call(
        functools.partial(_ret_proj_body, rope=rope, qk=qk),
        operands,
        grid=(nb, t // tm),
        out_specs=[tok(qk), tok(qk), tok(vd), tok(vd)],
        out_shape=[jax.ShapeDtypeStruct((nb, t, w), BF16) for w in (qk, qk, vd, vd)],
        compiler_params=_params("parallel", "parallel"),
        name="ret_proj",
    )


def _ret_scan_body(*refs, reverse, chunk, finish):
    if finish:
        dec_ref, q_ref, k_ref, v_ref, s0_ref, yf_ref, g_ref, y_ref, sfin_ref, s_scr = refs
    else:
        dec_ref, q_ref, k_ref, v_ref, s0_ref, y_ref, sfin_ref, s_scr = refs
    t = pl.program_id(1)

    @pl.when(t == 0)
    def _():
        s_scr[...] = s0_ref[...]

    c = chunk
    n_chunks = q_ref.shape[0] // c
    dk, dv = s_scr.shape[1:]
    row = lax.broadcasted_iota(jnp.int32, (c, c), 0).astype(F32)
    col = lax.broadcasted_iota(jnp.int32, (c, c), 1).astype(F32)
    pos = lax.broadcasted_iota(jnp.int32, (c, 1), 0).astype(F32)
    diff = col - row if reverse else row - col
    decays = []
    for hd in range(HEADS):
        lg = -jnp.abs(dec_ref[hd][:, 0:1])
        if reverse:
            q_dec, k_dec = jnp.exp(lg * (c - pos)), jnp.exp(lg * pos)
        else:
            q_dec, k_dec = jnp.exp(lg * (pos + 1.0)), jnp.exp(lg * (c - 1.0 - pos))
        intra = jnp.where(diff >= 0, jnp.exp(lg * jnp.maximum(diff, 0.0)), 0.0)
        decays.append((q_dec, k_dec, intra, jnp.exp(lg * c)))

    for ci in (range(n_chunks - 1, -1, -1) if reverse else range(n_chunks)):
        sl = pl.ds(ci * c, c)
        for hd in range(HEADS):
            q_dec, k_dec, intra, chunk_dec = decays[hd]
            qs, vs = pl.ds(hd * dk, dk), pl.ds(hd * dv, dv)
            q = q_ref[sl, qs]
            k = k_ref[sl, qs]
            v = v_ref[sl, vs]
            s = s_scr[hd]
            scores = lax.dot_general(q, k, (((1,), (1,)), ((), ())), preferred_element_type=F32) * intra
            y = _dot(scores.astype(BF16), v) + _dot((q.astype(F32) * q_dec).astype(BF16), s.astype(BF16))
            kd = (k.astype(F32) * k_dec).astype(BF16)
            s_scr[hd] = chunk_dec * s + lax.dot_general(kd, v, (((0,), (0,)), ((), ())),
                                                        preferred_element_type=F32)
            if finish:
                y = y + yf_ref[sl, vs]
                mu = jnp.mean(y, axis=-1, keepdims=True)
                yc = y - mu
                yn = yc * lax.rsqrt(jnp.mean(yc * yc, axis=-1, keepdims=True) + EPS)
                y_ref[sl, vs] = (_silu(g_ref[sl, vs].astype(F32)) * yn).astype(BF16)
            else:
                y_ref[sl, vs] = y

    @pl.when(t == pl.num_programs(1) - 1)
    def _():
        sfin_ref[...] = s_scr[...]


def _ret_scan(dec, q, k, v, s0, yf=None, g=None, *, reverse, tb, chunk):
    nb, t, qk = q.shape
    vd = v.shape[-1]
    dk, dv = qk // HEADS, vd // HEADS
    nt = t // tb
    finish = yf is not None
    tsel = (lambda i: nt - 1 - i) if reverse else (lambda i: i)
    tokspec = lambda w: pl.BlockSpec((None, tb, w), lambda b, i: (b, tsel(i), 0))
    state = pl.BlockSpec((None, HEADS, dk, dv), lambda b, i: (b, 0, 0, 0))
    operands = [_whole(dec), (tokspec(qk), q), (tokspec(qk), k), (tokspec(vd), v), (state, s0)]
    if finish:
        operands += [(tokspec(vd), yf), (tokspec(vd), g)]
    return _call(
        functools.partial(_ret_scan_body, reverse=reverse, chunk=chunk, finish=finish),
        operands,
        grid=(nb, nt),
        out_specs=[tokspec(vd), state],
        out_shape=[jax.ShapeDtypeStruct((nb, t, vd), BF16 if finish else F32),
                   jax.ShapeDtypeStruct((nb, HEADS, dk, dv), F32)],
        scratch_shapes=[pltpu.VMEM((HEADS, dk, dv), F32)],
        compiler_params=_params("parallel", "arbitrary"),
        name="ret_scan_bwd" if reverse else "ret_scan_fwd",
    )


def _ret_out_body(x_ref, mod_ref, ng_ref, a_ref, w_ref, o_ref):
    del ng_ref
    m = mod_ref[...]
    o_ref[...] = x_ref[...] + m[5:6] * _dot(a_ref[...], w_ref[...])


def _ret_out(x, lay, a, w_out, slot, tm=512):
    nb, t, d = x.shape
    vd = a.shape[-1]
    tok = lambda w: pl.BlockSpec((None, tm, w), lambda b, i: (b, i, 0))
    operands = [(tok(d), x)] + _layer_blocks(lay) + [(tok(vd), a), _whole(w_out, slot)]
    return _call(
        _ret_out_body,
        operands,
        grid=(nb, t // tm),
        out_specs=tok(d),
        out_shape=jax.ShapeDtypeStruct(x.shape, F32),
        compiler_params=_params("parallel", "parallel"),
        name="ret_out",
    )


def _axial_rope(n_tokens, dk):
    rows = n_tokens // GRID_W
    row = jnp.repeat(jnp.arange(rows, dtype=F32), GRID_W)
    col = jnp.tile(jnp.arange(GRID_W, dtype=F32), rows)
    n_freq = dk // 4
    inv_freq = ROPE_BASE ** (-jnp.arange(n_freq, dtype=F32) / n_freq)
    ang = jnp.concatenate([row[:, None] * inv_freq, col[:, None] * inv_freq], axis=-1)
    return jnp.cos(ang), jnp.sin(ang)


def _retention(x, ctx, lat, cx, w_in, w_out, slot, decay, cos, sin, need_ctx_out):
    nb, n, d = x.shape
    n_ctx = ctx.shape[1]
    qk = d
    vd = w_out.shape[1]
    dk, dv = qk // HEADS, vd // HEADS
    dec = jnp.broadcast_to(decay[:, :, None, None], (2, HEADS, 1, LANES))

    ctx_flat = ctx.reshape(1, nb * n_ctx, d)
    qc, kc, vc, gc = [a.reshape(nb, n_ctx, -1) for a in _ret_proj(ctx_flat, cx, w_in, slot, qk=qk, vd=vd)]
    zeros = jnp.zeros((nb, HEADS, dk, dv), F32)
    yc_f, s_f = _ret_scan(dec[0], qc, kc, vc, zeros, reverse=False, tb=n_ctx, chunk=RET_CHUNK)
    ac, s_b = _ret_scan(dec[1], qc, kc, vc, zeros, yc_f, gc, reverse=True, tb=n_ctx, chunk=RET_CHUNK)

    ql, kl, vl, gl = _ret_proj(x, lat, w_in, slot, cos, sin, qk=qk, vd=vd)
    yl_f, _ = _ret_scan(dec[0], ql, kl, vl, s_f, reverse=False, tb=RET_BLOCK_FWD, chunk=RET_CHUNK)
    al, _ = _ret_scan(dec[1], ql, kl, vl, s_b, yl_f, gl, reverse=True, tb=RET_BLOCK_BWD, chunk=RET_CHUNK)
    x = _ret_out(x, lat, al, w_out, slot)
    if need_ctx_out:
        ctx = _ret_out(ctx_flat, cx, ac.reshape(1, nb * n_ctx, vd), w_out, slot).reshape(ctx.shape)
    return x, ctx


def kernel(x, c, ctx, c_ctx, ada_w, ada_b, norm_g, final_g, ffn_w_gate, ffn_w_up, ffn_w_down,
           four_w, four_b, ret_w_in, ret_w_out, ret_decay):
    nb, n, d = x.shape
    n_ctx = ctx.shape[1]
    depth = ada_w.shape[0]
    n1 = int(round(n ** 0.5))
    assert n1 * n1 == n and nb + 1 <= MOD_ROWS

    cvec = jnp.zeros((MOD_ROWS, d), F32).at[:nb].set(c).at[nb].set(c_ctx)
    mods = _mods(cvec, ada_w, ada_b).reshape(depth, MOD_ROWS, N_MOD, d)
    w3 = tuple(w.astype(BF16) for w in (ffn_w_gate, ffn_w_up, ffn_w_down))
    four_wb, w_in, w_out = (w.astype(BF16) for w in (four_w, ret_w_in, ret_w_out))
    four_b3 = four_b.reshape(four_b.shape[0], 1, d)
    ftab, m2, mctx, cd, sd = _dft_tables(n, n_ctx, d // N_GROUPS)
    cos, sin = _axial_rope(n, d // HEADS)
    flat = lambda a: a.reshape(1, nb * n_ctx, d)

    for l in range(depth):
        last = l == depth - 1
        mixer, slot = l % 2, l // 2
        ctx_read = (not last) or mixer == 1
        lat = Layer(mods, norm_g, l, None)
        cx = Layer(mods, norm_g, l, nb)

        if mixer == 0:
            x, h = _ffn(x, lat, w3, 0, j=0, emit_h=True)
        else:
            x = _ffn(x, lat, w3, 0, j=0)
        if ctx_read:
            ctx = _ffn(flat(ctx), cx, w3, 0, j=0).reshape(ctx.shape)

        y_add = None
        if mixer == 0:
            b1 = _four1(h.reshape(nb, n1, n1, d), ftab)
            y_add = _four2(b1, m2, cd, sd, four_wb, four_b3, slot).reshape(nb, n, d)
            if ctx_read:
                ctx = _four_ctx(ctx, cx, mctx, cd, sd, four_wb, four_b3, slot)
        else:
            x, ctx = _retention(x, ctx, lat, cx, w_in, w_out, slot, ret_decay[slot], cos, sin, not last)

        x = _ffn(x, lat, w3, 1, j=2, add=y_add, final_g=final_g if last else None)
        if not last:
            ctx = _ffn(flat(ctx), cx, w3, 1, j=2).reshape(ctx.shape)
    return x
```

```python
import functools
from typing import NamedTuple

import jax
import jax.numpy as jnp
import numpy as np
from jax import lax
from jax.experimental import pallas as pl
from jax.experimental.pallas import tpu as pltpu

F32 = jnp.float32
BF16 = jnp.bfloat16

GRID_W = 64
N_GROUPS = 4
HEADS = 4
RET_CHUNK = 256
RET_BLOCK_FWD = 1024
RET_BLOCK_BWD = 512
FOUR_GROUP = 16
FFN_TILE = 1024
ROPE_BASE = 10000.0
N_MOD = 9
EPS = 1e-6
LANES = 128
MOD_ROWS = 8


class Layer(NamedTuple):
    mods: jax.Array
    norm_g: jax.Array
    index: int
    row: int | None


def _dot(a, b):
    return jnp.dot(a, b, preferred_element_type=F32)


def _rms(x, g):
    return x * lax.rsqrt(jnp.mean(x * x, axis=-1, keepdims=True) + EPS) * g


def _prenorm_mod(x, g, shift, scale):
    return _rms(x, g) * (1.0 + scale) + shift


def _silu(x):
    return x * jax.nn.sigmoid(x)


def _whole(arr, *idx):
    shape = (None,) * len(idx) + tuple(arr.shape[len(idx):])
    tail = (0,) * (arr.ndim - len(idx))
    return pl.BlockSpec(shape, lambda *_: idx + tail, pipeline_mode=pl.Buffered(1)), arr


def _layer_blocks(lay):
    d = lay.mods.shape[-1]
    l, row = lay.index, lay.row
    pick = (lambda b, *_: (l, b, 0, 0)) if row is None else (lambda *_: (l, row, 0, 0))
    return [(pl.BlockSpec((None, None, N_MOD, d), pick), lay.mods), _whole(lay.norm_g, l)]


def _call(body, operands, **kw):
    specs, args = zip(*operands)
    return pl.pallas_call(body, in_specs=list(specs), **kw)(*args)


def _params(*sem):
    return pltpu.CompilerParams(dimension_semantics=sem)


def _mods_body(c_ref, w_ref, b_ref, o_ref):
    s = _silu(c_ref[...]).astype(BF16)
    o_ref[...] = _dot(s, w_ref[...].astype(BF16)) + b_ref[...]


def _mods(cvec, ada_w, ada_b):
    depth, d, nd = ada_w.shape
    tn = d
    return pl.pallas_call(
        _mods_body,
        grid=(depth, nd // tn),
        in_specs=[
            pl.BlockSpec((MOD_ROWS, d), lambda l, n: (0, 0)),
            pl.BlockSpec((None, d, tn), lambda l, n: (l, 0, n)),
            pl.BlockSpec((None, 1, tn), lambda l, n: (l, 0, n)),
        ],
        out_specs=pl.BlockSpec((None, MOD_ROWS, tn), lambda l, n: (l, 0, n)),
        out_shape=jax.ShapeDtypeStruct((depth, MOD_ROWS, nd), F32),
        compiler_params=_params("parallel", "parallel"),
        name="mods",
    )(cvec, ada_w, ada_b.reshape(depth, 1, nd))


def _ffn_body(*refs, j, has_add, final, emit_h):
    x_ref, mod_ref, ng_ref, wg_ref, wu_ref, wd_ref = refs[:6]
    rest = list(refs[6:])
    add_ref = rest.pop(0) if has_add else None
    fg_ref = rest.pop(0) if final else None
    o_ref = rest.pop(0)
    x = x_ref[...]
    m = mod_ref[...]
    ng = ng_ref[...]
    if has_add:
        x = x + m[5:6] * add_ref[...].astype(F32)
    h = _prenorm_mod(x, ng[j:j + 1], m[3 * j:3 * j + 1], m[3 * j + 1:3 * j + 2]).astype(BF16)
    a = (_silu(_dot(h, wg_ref[...])) * _dot(h, wu_ref[...])).astype(BF16)
    out = x + (0.5 * m[3 * j + 2:3 * j + 3]) * _dot(a, wd_ref[...])
    if emit_h:
        rest.pop(0)[...] = _prenorm_mod(out, ng[1:2], m[3:4], m[4:5]).astype(BF16)
    if final:
        out = _rms(out, fg_ref[...])
    o_ref[...] = out


def _ffn(x, lay, w3, jj, *, j, add=None, final_g=None, emit_h=False):
    nb, t, d = x.shape
    tm = min(t, FFN_TILE)
    tok = pl.BlockSpec((None, tm, d), lambda b, i: (b, i, 0))
    operands = [(tok, x)] + _layer_blocks(lay) + [_whole(w, lay.index, jj) for w in w3]
    if add is not None:
        operands.append((tok, add))
    if final_g is not None:
        operands.append(_whole(final_g.reshape(1, d)))
    out_shape = [jax.ShapeDtypeStruct(x.shape, F32)] + ([jax.ShapeDtypeStruct(x.shape, BF16)] if emit_h else [])
    out = _call(
        functools.partial(_ffn_body, j=j, has_add=add is not None, final=final_g is not None, emit_h=emit_h),
        operands,
        grid=(nb, t // tm),
        out_specs=[tok] * len(out_shape),
        out_shape=out_shape,
        compiler_params=_params("parallel", "parallel"),
        name="ffn",
    )
    return out if emit_h else out[0]


def _four1_body(h_ref, f_ref, o_ref):
    hs = jnp.swapaxes(h_ref[...], 0, 1)
    for g in range(hs.shape[0]):
        o_ref[g] = _dot(f_ref[g], hs[g]).astype(BF16)


def _four1(h4, ftab, group=FOUR_GROUP):
    nb, n1, n2, d = h4.shape
    return pl.pallas_call(
        _four1_body,
        grid=(nb, n2 // group),
        in_specs=[pl.BlockSpec((None, n1, group, d), lambda b, i: (b, 0, i, 0)),
                  pl.BlockSpec((group, 2 * n1, n1), lambda b, i: (i, 0, 0))],
        out_specs=pl.BlockSpec((None, group, 2 * n1, d), lambda b, i: (b, i, 0, 0)),
        out_shape=jax.ShapeDtypeStruct((nb, n2, 2 * n1, d), BF16),
        compiler_params=_params("parallel", "parallel"),
        name="four1",
    )(h4, ftab)


def _channel_mix(xre, xim, cd, sd, w, b):
    gd = cd.shape[0]
    cols = [_dot(xre[:, s:s + gd], cd) + _dot(xim[:, s:s + gd], sd) for s in range(0, xre.shape[1], gd)]
    return _dot(jnp.concatenate(cols, axis=1).astype(BF16), w) + b


def _four2_body(re_ref, im_ref, m2_ref, cd_ref, sd_ref, w_ref, b_ref, o_ref, *, sub):
    n2, group, d = re_ref.shape
    bre = jnp.swapaxes(re_ref[...], 0, 1)
    bim = jnp.swapaxes(im_ref[...], 0, 1)
    ys = []
    for g0 in range(0, group, sub):
        res, ims = [], []
        for g in range(g0, g0 + sub):
            xx = _dot(m2_ref[...], jnp.concatenate([bre[g], bim[g]], axis=0))
            res.append(xx[:n2])
            ims.append(xx[n2:])
        xre = jnp.concatenate(res, axis=0).astype(BF16)
        xim = jnp.concatenate(ims, axis=0).astype(BF16)
        ys.append(_channel_mix(xre, xim, cd_ref[...], sd_ref[...], w_ref[...], b_ref[...]).astype(BF16))
    y = jnp.concatenate(ys, axis=0).reshape(group, n2, d)
    o_ref[...] = jnp.swapaxes(y, 0, 1)


def _four2(b1, m2, cd, sd, w, b, slot, group=FOUR_GROUP, sub=4):
    nb, n2, k1x2, d = b1.shape
    k1 = k1x2 // 2
    part = lambda off: pl.BlockSpec((None, n2, group, d), lambda bb, i: (bb, 0, off + i, 0))
    operands = [(part(0), b1), (part(k1 // group), b1),
                _whole(m2), _whole(cd), _whole(sd), _whole(w, slot), _whole(b, slot)]
    return _call(
        functools.partial(_four2_body, sub=sub),
        operands,
        grid=(nb, k1 // group),
        out_specs=pl.BlockSpec((None, n2, group, d), lambda bb, i: (bb, 0, i, 0)),
        out_shape=jax.ShapeDtypeStruct((nb, n2, k1, d), BF16),
        compiler_params=_params("parallel", "parallel"),
        name="four2",
    )


def _four_ctx_body(x_ref, mod_ref, ng_ref, m_ref, cd_ref, sd_ref, w_ref, b_ref, o_ref):
    x = x_ref[...]
    m = mod_ref[...]
    h = _prenorm_mod(x, ng_ref[1:2], m[3:4], m[4:5]).astype(BF16)
    n = x.shape[0]
    xx = _dot(m_ref[...], h)
    y = _channel_mix(xx[:n].astype(BF16), xx[n:].astype(BF16), cd_ref[...], sd_ref[...], w_ref[...], b_ref[...])
    o_ref[...] = x + m[5:6] * y


def _four_ctx(ctx, lay, mseq, cd, sd, w, b, slot):
    nb, n, d = ctx.shape
    tok = pl.BlockSpec((None, n, d), lambda bb: (bb, 0, 0))
    operands = [(tok, ctx)] + _layer_blocks(lay) + [_whole(mseq), _whole(cd), _whole(sd),
                                                    _whole(w, slot), _whole(b, slot)]
    return _call(
        _four_ctx_body,
        operands,
        grid=(nb,),
        out_specs=tok,
        out_shape=jax.ShapeDtypeStruct(ctx.shape, F32),
        compiler_params=_params("parallel"),
        name="four_ctx",
    )


def _dft_tables(n_lat, n_ctx, gd):
    n1 = int(round(n_lat ** 0.5))
    two_pi = 2.0 * np.pi
    k1 = jnp.arange(n1, dtype=jnp.int32)
    pos = (jnp.arange(n1, dtype=jnp.int32)[None, :] * n1 + jnp.arange(n1, dtype=jnp.int32)[:, None])
    r = (k1[None, :, None] * pos[:, None, :]) % n_lat
    ang = r.astype(F32) * (two_pi / n_lat)
    ftab = jnp.concatenate([jnp.cos(ang), -jnp.sin(ang)], axis=1).astype(BF16)

    def cs(n, scale):
        i = jnp.arange(n, dtype=jnp.int32)
        a = ((i[:, None] * i[None, :]) % n).astype(F32) * (two_pi / n)
        return jnp.cos(a) * scale, jnp.sin(a) * scale

    c2, s2 = cs(n1, 1.0 / n1)
    m2 = jnp.concatenate([jnp.concatenate([c2, s2], axis=1),
                          jnp.concatenate([-s2, c2], axis=1)], axis=0).astype(BF16)
    cc, sc = cs(n_ctx, n_ctx ** -0.5)
    mctx = jnp.concatenate([cc, -sc], axis=0).astype(BF16)
    cd, sd = cs(gd, gd ** -0.5)
    return ftab, m2, mctx, cd.astype(BF16), sd.astype(BF16)


def _ret_proj_body(*refs, rope, qk):
    if rope:
        x_ref, mod_ref, ng_ref, w_ref, cos_ref, sin_ref, q_ref, k_ref, v_ref, g_ref = refs
    else:
        x_ref, mod_ref, ng_ref, w_ref, q_ref, k_ref, v_ref, g_ref = refs
    m = mod_ref[...]
    h = _prenorm_mod(x_ref[...], ng_ref[1:2], m[3:4], m[4:5]).astype(BF16)
    dk = qk // HEADS
    half = dk // 2
    for src, dst, scale in ((0, q_ref, 1.0), (qk, k_ref, dk ** -0.5)):
        t = _dot(h, w_ref[:, src:src + qk]) * scale
        for s in range(0, qk, dk):
            t1 = t[:, s:s + half]
            t2 = t[:, s + half:s + dk]
            if rope:
                cos = cos_ref[...]
                sin = sin_ref[...]
                t1, t2 = t1 * cos - t2 * sin, t2 * cos + t1 * sin
            dst[:, s:s + half] = t1.astype(BF16)
            dst[:, s + half:s + dk] = t2.astype(BF16)
    vd = v_ref.shape[-1]
    v_ref[...] = _dot(h, w_ref[:, 2 * qk:2 * qk + vd]).astype(BF16)
    g_ref[...] = _silu(_dot(h, w_ref[:, 2 * qk + vd:2 * qk + 2 * vd])).astype(BF16)


def _ret_proj(x, lay, w_in, slot, cos=None, sin=None, *, qk, vd, tm=512):
    nb, t, d = x.shape
    rope = cos is not None
    tok = lambda width: pl.BlockSpec((None, tm, width), lambda b, i: (b, i, 0))
    operands = [(tok(d), x)] + _layer_blocks(lay) + [_whole(w_in, slot)]
    if rope:
        pos = pl.BlockSpec((tm, cos.shape[1]), lambda b, i: (i, 0))
        operands += [(pos, cos), (pos, sin)]
    return _call(
        functools.partial(_ret_proj_body, rope=rope, qk=qk),
        operands,
        grid=(nb, t // tm),
        out_specs=[tok(qk), tok(qk), tok(vd), tok(vd)],
        out_shape=[jax.ShapeDtypeStruct((nb, t, w), BF16) for w in (qk, qk, vd, vd)],
        compiler_params=_params("parallel", "parallel"),
        name="ret_proj",
    )


def _ret_scan_body(*refs, reverse, chunk, finish):
    if finish:
        dec_ref, q_ref, k_ref, v_ref, s0_ref, yf_ref, g_ref, x_ref, mod_ref, wo_ref, y_ref, sfin_ref, s_scr = refs
    else:
        dec_ref, q_ref, k_ref, v_ref, s0_ref, y_ref, sfin_ref, s_scr = refs
    t = pl.program_id(1)

    @pl.when(t == 0)
    def _():
        s_scr[...] = s0_ref[...]

    c = chunk
    n_chunks = q_ref.shape[0] // c
    dk, dv = s_scr.shape[1:]
    row = lax.broadcasted_iota(jnp.int32, (c, c), 0).astype(F32)
    col = lax.broadcasted_iota(jnp.int32, (c, c), 1).astype(F32)
    pos = lax.broadcasted_iota(jnp.int32, (c, 1), 0).astype(F32)
    diff = col - row if reverse else row - col
    decays = []
    for hd in range(HEADS):
        lg = -jnp.abs(dec_ref[hd][:, 0:1])
        if reverse:
            q_dec, k_dec = jnp.exp(lg * (c - pos)), jnp.exp(lg * pos)
        else:
            q_dec, k_dec = jnp.exp(lg * (pos + 1.0)), jnp.exp(lg * (c - 1.0 - pos))
        intra = jnp.where(diff >= 0, jnp.exp(lg * jnp.maximum(diff, 0.0)), 0.0)
        decays.append((q_dec, k_dec, intra, jnp.exp(lg * c)))

    for ci in (range(n_chunks - 1, -1, -1) if reverse else range(n_chunks)):
        sl = pl.ds(ci * c, c)
        gated = []
        for hd in range(HEADS):
            q_dec, k_dec, intra, chunk_dec = decays[hd]
            qs, vs = pl.ds(hd * dk, dk), pl.ds(hd * dv, dv)
            q = q_ref[sl, qs]
            k = k_ref[sl, qs]
            v = v_ref[sl, vs]
            s = s_scr[hd]
            scores = lax.dot_general(q, k, (((1,), (1,)), ((), ())), preferred_element_type=F32) * intra
            y = _dot(scores.astype(BF16), v) + _dot((q.astype(F32) * q_dec).astype(BF16), s.astype(BF16))
            kd = (k.astype(F32) * k_dec).astype(BF16)
            s_scr[hd] = chunk_dec * s + lax.dot_general(kd, v, (((0,), (0,)), ((), ())),
                                                        preferred_element_type=F32)
            if finish:
                y = y + yf_ref[sl, vs]
                mu = jnp.mean(y, axis=-1, keepdims=True)
                yc = y - mu
                yn = yc * lax.rsqrt(jnp.mean(yc * yc, axis=-1, keepdims=True) + EPS)
                gated.append((g_ref[sl, vs].astype(F32) * yn).astype(BF16))
            else:
                y_ref[sl, vs] = y
        if finish:
            mixed = _dot(jnp.concatenate(gated, axis=1), wo_ref[...])
            y_ref[sl, :] = x_ref[sl, :] + mod_ref[5:6] * mixed

    @pl.when(t == pl.num_programs(1) - 1)
    def _():
        sfin_ref[...] = s_scr[...]


def _ret_scan(dec, q, k, v, s0, finish=None, *, reverse, tb, chunk):
    nb, t, qk = q.shape
    vd = v.shape[-1]
    dk, dv = qk // HEADS, vd // HEADS
    nt = t // tb
    tsel = (lambda i: nt - 1 - i) if reverse else (lambda i: i)
    tokspec = lambda w: pl.BlockSpec((None, tb, w), lambda b, i: (b, tsel(i), 0))
    state = pl.BlockSpec((None, HEADS, dk, dv), lambda b, i: (b, 0, 0, 0))
    operands = [_whole(dec), (tokspec(qk), q), (tokspec(qk), k), (tokspec(vd), v), (state, s0)]
    y_shape = jax.ShapeDtypeStruct((nb, t, vd), F32)
    if finish is not None:
        yf, gate, x, lay, w_out, slot = finish
        d = x.shape[-1]
        operands += [(tokspec(vd), yf), (tokspec(vd), gate), (tokspec(d), x), _layer_blocks(lay)[0],
                     _whole(w_out, slot)]
        y_shape = jax.ShapeDtypeStruct(x.shape, F32)
    return _call(
        functools.partial(_ret_scan_body, reverse=reverse, chunk=chunk, finish=finish is not None),
        operands,
        grid=(nb, nt),
        out_specs=[tokspec(y_shape.shape[-1]), state],
        out_shape=[y_shape, jax.ShapeDtypeStruct((nb, HEADS, dk, dv), F32)],
        scratch_shapes=[pltpu.VMEM((HEADS, dk, dv), F32)],
        compiler_params=_params("parallel", "arbitrary"),
        name="ret_scan_bwd" if reverse else "ret_scan_fwd",
    )


def _axial_rope(n_tokens, dk):
    rows = n_tokens // GRID_W
    row = jnp.repeat(jnp.arange(rows, dtype=F32), GRID_W)
    col = jnp.tile(jnp.arange(GRID_W, dtype=F32), rows)
    n_freq = dk // 4
    inv_freq = ROPE_BASE ** (-jnp.arange(n_freq, dtype=F32) / n_freq)
    ang = jnp.concatenate([row[:, None] * inv_freq, col[:, None] * inv_freq], axis=-1)
    return jnp.cos(ang), jnp.sin(ang)


def _retention(x, ctx, lat, cx, w_in, w_out, slot, decay, cos, sin):
    nb, n, d = x.shape
    n_ctx = ctx.shape[1]
    qk = d
    vd = w_out.shape[1]
    dk, dv = qk // HEADS, vd // HEADS
    dec = jnp.broadcast_to(decay[:, :, None, None], (2, HEADS, 1, LANES))
    scan = functools.partial(_ret_scan, chunk=RET_CHUNK)

    ctx_flat = ctx.reshape(1, nb * n_ctx, d)
    qc, kc, vc, gc = [a.reshape(nb, n_ctx, -1) for a in _ret_proj(ctx_flat, cx, w_in, slot, qk=qk, vd=vd)]
    zeros = jnp.zeros((nb, HEADS, dk, dv), F32)
    yc_f, s_f = scan(dec[0], qc, kc, vc, zeros, reverse=False, tb=n_ctx)
    ctx, s_b = scan(dec[1], qc, kc, vc, zeros, (yc_f, gc, ctx, cx, w_out, slot), reverse=True, tb=n_ctx)

    ql, kl, vl, gl = _ret_proj(x, lat, w_in, slot, cos, sin, qk=qk, vd=vd)
    yl_f, _ = scan(dec[0], ql, kl, vl, s_f, reverse=False, tb=RET_BLOCK_FWD)
    x, _ = scan(dec[1], ql, kl, vl, s_b, (yl_f, gl, x, lat, w_out, slot), reverse=True, tb=RET_BLOCK_BWD)
    return x, ctx


def kernel(x, c, ctx, c_ctx, ada_w, ada_b, norm_g, final_g, ffn_w_gate, ffn_w_up, ffn_w_down,
           four_w, four_b, ret_w_in, ret_w_out, ret_decay):
    nb, n, d = x.shape
    n_ctx = ctx.shape[1]
    depth = ada_w.shape[0]
    n1 = int(round(n ** 0.5))
    assert n1 * n1 == n and nb + 1 <= MOD_ROWS

    cvec = jnp.zeros((MOD_ROWS, d), F32).at[:nb].set(c).at[nb].set(c_ctx)
    mods = _mods(cvec, ada_w, ada_b).reshape(depth, MOD_ROWS, N_MOD, d)
    w3 = tuple(w.astype(BF16) for w in (ffn_w_gate, ffn_w_up, ffn_w_down))
    four_wb, w_in, w_out = (w.astype(BF16) for w in (four_w, ret_w_in, ret_w_out))
    four_b3 = four_b.reshape(four_b.shape[0], 1, d)
    ftab, m2, mctx, cd, sd = _dft_tables(n, n_ctx, d // N_GROUPS)
    cos, sin = _axial_rope(n, d // HEADS)
    flat = lambda a: a.reshape(1, nb * n_ctx, d)

    for l in range(depth):
        last = l == depth - 1
        mixer, slot = l % 2, l // 2
        ctx_read = (not last) or mixer == 1
        lat = Layer(mods, norm_g, l, None)
        cx = Layer(mods, norm_g, l, nb)

        if mixer == 0:
            x, h = _ffn(x, lat, w3, 0, j=0, emit_h=True)
        else:
            x = _ffn(x, lat, w3, 0, j=0)
        if ctx_read:
            ctx = _ffn(flat(ctx), cx, w3, 0, j=0).reshape(ctx.shape)

        y_add = None
        if mixer == 0:
            b1 = _four1(h.reshape(nb, n1, n1, d), ftab)
            y_add = _four2(b1, m2, cd, sd, four_wb, four_b3, slot).reshape(nb, n, d)
            if ctx_read:
                ctx = _four_ctx(ctx, cx, mctx, cd, sd, four_wb, four_b3, slot)
        else:
            x, new_ctx = _retention(x, ctx, lat, cx, w_in, w_out, slot, ret_decay[slot], cos, sin)
            ctx = ctx if last else new_ctx

        x = _ffn(x, lat, w3, 1, j=2, add=y_add, final_g=final_g if last else None)
        if not last:
            ctx = _ffn(flat(ctx), cx, w3, 1, j=2).reshape(ctx.shape)
    return x
```

```python
import functools
from typing import NamedTuple

import jax
import jax.numpy as jnp
import numpy as np
from jax import lax
from jax.experimental import pallas as pl
from jax.experimental.pallas import tpu as pltpu

F32 = jnp.float32
BF16 = jnp.bfloat16

GRID_W = 64
N_GROUPS = 4
HEADS = 4
RET_CHUNK = 256
RET_BLOCK_FWD = 1024
RET_BLOCK_BWD = 512
FOUR_GROUP = 16
FFN_TILE = 1024
ROPE_BASE = 10000.0
N_MOD = 9
EPS = 1e-6
LANES = 128
MOD_ROWS = 8


class Layer(NamedTuple):
    mods: jax.Array
    norm_g: jax.Array
    index: int
    row: int | None


def _dot(a, b):
    return jnp.dot(a, b, preferred_element_type=F32)


def _rms(x, g):
    return x * lax.rsqrt(jnp.mean(x * x, axis=-1, keepdims=True) + EPS) * g


def _prenorm_mod(x, g, shift, scale):
    return _rms(x, g) * (1.0 + scale) + shift


def _silu(x):
    return x * jax.nn.sigmoid(x)


def _whole(arr, *idx):
    shape = (None,) * len(idx) + tuple(arr.shape[len(idx):])
    tail = (0,) * (arr.ndim - len(idx))
    return pl.BlockSpec(shape, lambda *_: idx + tail, pipeline_mode=pl.Buffered(1)), arr


def _layer_blocks(lay):
    d = lay.mods.shape[-1]
    l, row = lay.index, lay.row
    pick = (lambda b, *_: (l, b, 0, 0)) if row is None else (lambda *_: (l, row, 0, 0))
    return [(pl.BlockSpec((None, None, N_MOD, d), pick), lay.mods), _whole(lay.norm_g, l)]


def _call(body, operands, **kw):
    specs, args = zip(*operands)
    return pl.pallas_call(body, in_specs=list(specs), **kw)(*args)


def _params(*sem):
    return pltpu.CompilerParams(dimension_semantics=sem)


def _mods_body(c_ref, w_ref, b_ref, o_ref):
    s = _silu(c_ref[...]).astype(BF16)
    o_ref[...] = _dot(s, w_ref[...].astype(BF16)) + b_ref[...]


def _mods(cvec, ada_w, ada_b):
    depth, d, nd = ada_w.shape
    tn = d
    return pl.pallas_call(
        _mods_body,
        grid=(depth, nd // tn),
        in_specs=[
            pl.BlockSpec((MOD_ROWS, d), lambda l, n: (0, 0)),
            pl.BlockSpec((None, d, tn), lambda l, n: (l, 0, n)),
            pl.BlockSpec((None, 1, tn), lambda l, n: (l, 0, n)),
        ],
        out_specs=pl.BlockSpec((None, MOD_ROWS, tn), lambda l, n: (l, 0, n)),
        out_shape=jax.ShapeDtypeStruct((depth, MOD_ROWS, nd), F32),
        compiler_params=_params("parallel", "parallel"),
        name="mods",
    )(cvec, ada_w, ada_b.reshape(depth, 1, nd))


def _swiglu_step(x, m, ng, wg_ref, wu_ref, wd_ref, j):
    h = _prenorm_mod(x, ng[j:j + 1], m[3 * j:3 * j + 1], m[3 * j + 1:3 * j + 2]).astype(BF16)
    a = (_silu(_dot(h, wg_ref[...])) * _dot(h, wu_ref[...])).astype(BF16)
    return x + (0.5 * m[3 * j + 2:3 * j + 3]) * _dot(a, wd_ref[...])


def _ffn_body(*refs, j, has_add, final, emit_h, with_ctx, lat_steps):
    refs = list(refs)
    x_ref, mod_ref = refs.pop(0), refs.pop(0)
    c_ref, cmod_ref = (refs.pop(0), refs.pop(0)) if with_ctx else (None, None)
    ng_ref, wg_ref, wu_ref, wd_ref = (refs.pop(0) for _ in range(4))
    add_ref = refs.pop(0) if has_add else None
    fg_ref = refs.pop(0) if final else None
    o_ref = refs.pop(0)
    co_ref = refs.pop(0) if with_ctx else None
    hn_ref = refs.pop(0) if emit_h else None
    ng = ng_ref[...]

    def latent():
        x = x_ref[...]
        m = mod_ref[...]
        if has_add:
            x = x + m[5:6] * add_ref[...].astype(F32)
        out = _swiglu_step(x, m, ng, wg_ref, wu_ref, wd_ref, j)
        if emit_h:
            hn_ref[...] = _prenorm_mod(out, ng[1:2], m[3:4], m[4:5]).astype(BF16)
        o_ref[...] = _rms(out, fg_ref[...]) if final else out

    if with_ctx:
        pl.when(pl.program_id(0) < lat_steps)(latent)

        @pl.when(pl.program_id(0) == lat_steps)
        def _():
            co_ref[...] = _swiglu_step(c_ref[...], cmod_ref[...], ng, wg_ref, wu_ref, wd_ref, j)
    else:
        latent()


def _ffn(x, lay, w3, jj, *, j, ctx=None, add=None, final_g=None, emit_h=False):
    nb, t, d = x.shape
    tm = min(t, FFN_TILE)
    nt = t // tm
    lat_steps = nb * nt
    l = lay.index
    tile = lambda g: jnp.minimum(g, lat_steps - 1)
    tok = pl.BlockSpec((None, tm, d), lambda g: (tile(g) // nt, tile(g) % nt, 0))
    operands = [(tok, x), (pl.BlockSpec((None, None, N_MOD, d), lambda g: (l, tile(g) // nt, 0, 0)), lay.mods)]
    out_shape, out_specs = [jax.ShapeDtypeStruct(x.shape, F32)], [tok]
    if ctx is not None:
        cflat = ctx.reshape(1, -1, d)
        ctok = pl.BlockSpec((None,) + cflat.shape[1:], lambda g: (0, 0, 0))
        operands += [(ctok, cflat), (pl.BlockSpec((None, None, N_MOD, d), lambda g: (l, nb, 0, 0)), lay.mods)]
        out_shape.append(jax.ShapeDtypeStruct(cflat.shape, F32))
        out_specs.append(ctok)
    operands += [_whole(lay.norm_g, l)] + [_whole(w, l, jj) for w in w3]
    if add is not None:
        operands.append((tok, add))
    if final_g is not None:
        operands.append(_whole(final_g.reshape(1, d)))
    if emit_h:
        out_shape.append(jax.ShapeDtypeStruct(x.shape, BF16))
        out_specs.append(tok)
    out = list(_call(
        functools.partial(_ffn_body, j=j, has_add=add is not None, final=final_g is not None, emit_h=emit_h,
                          with_ctx=ctx is not None, lat_steps=lat_steps),
        operands,
        grid=(lat_steps + (ctx is not None),),
        out_specs=out_specs,
        out_shape=out_shape,
        compiler_params=_params("arbitrary"),
        name="ffn",
    ))
    x_new = out.pop(0)
    ctx_new = out.pop(0).reshape(ctx.shape) if ctx is not None else None
    return x_new, ctx_new, (out.pop(0) if emit_h else None)


def _four1_body(h_ref, f_ref, o_ref):
    hs = jnp.swapaxes(h_ref[...], 0, 1)
    for g in range(hs.shape[0]):
        o_ref[g] = _dot(f_ref[g], hs[g]).astype(BF16)


def _four1(h4, ftab, group=FOUR_GROUP):
    nb, n1, n2, d = h4.shape
    return pl.pallas_call(
        _four1_body,
        grid=(nb, n2 // group),
        in_specs=[pl.BlockSpec((None, n1, group, d), lambda b, i: (b, 0, i, 0)),
                  pl.BlockSpec((group, 2 * n1, n1), lambda b, i: (i, 0, 0))],
        out_specs=pl.BlockSpec((None, group, 2 * n1, d), lambda b, i: (b, i, 0, 0)),
        out_shape=jax.ShapeDtypeStruct((nb, n2, 2 * n1, d), BF16),
        compiler_params=_params("parallel", "parallel"),
        name="four1",
    )(h4, ftab)


def _channel_mix(xre, xim, cd, sd, w, b):
    gd = cd.shape[0]
    cols = [_dot(xre[:, s:s + gd], cd) + _dot(xim[:, s:s + gd], sd) for s in range(0, xre.shape[1], gd)]
    return _dot(jnp.concatenate(cols, axis=1).astype(BF16), w) + b


def _four2_body(re_ref, im_ref, m2_ref, cd_ref, sd_ref, w_ref, b_ref, o_ref, *, sub):
    n2, group, d = re_ref.shape
    bre = jnp.swapaxes(re_ref[...], 0, 1)
    bim = jnp.swapaxes(im_ref[...], 0, 1)
    ys = []
    for g0 in range(0, group, sub):
        res, ims = [], []
        for g in range(g0, g0 + sub):
            xx = _dot(m2_ref[...], jnp.concatenate([bre[g], bim[g]], axis=0))
            res.append(xx[:n2])
            ims.append(xx[n2:])
        xre = jnp.concatenate(res, axis=0).astype(BF16)
        xim = jnp.concatenate(ims, axis=0).astype(BF16)
        ys.append(_channel_mix(xre, xim, cd_ref[...], sd_ref[...], w_ref[...], b_ref[...]).astype(BF16))
    y = jnp.concatenate(ys, axis=0).reshape(group, n2, d)
    o_ref[...] = jnp.swapaxes(y, 0, 1)


def _four2(b1, m2, cd, sd, w, b, slot, group=FOUR_GROUP, sub=4):
    nb, n2, k1x2, d = b1.shape
    k1 = k1x2 // 2
    part = lambda off: pl.BlockSpec((None, n2, group, d), lambda bb, i: (bb, 0, off + i, 0))
    operands = [(part(0), b1), (part(k1 // group), b1),
                _whole(m2), _whole(cd), _whole(sd), _whole(w, slot), _whole(b, slot)]
    return _call(
        functools.partial(_four2_body, sub=sub),
        operands,
        grid=(nb, k1 // group),
        out_specs=pl.BlockSpec((None, n2, group, d), lambda bb, i: (bb, 0, i, 0)),
        out_shape=jax.ShapeDtypeStruct((nb, n2, k1, d), BF16),
        compiler_params=_params("parallel", "parallel"),
        name="four2",
    )


def _four_ctx_body(x_ref, mod_ref, ng_ref, m_ref, cd_ref, sd_ref, w_ref, b_ref, o_ref):
    x = x_ref[...]
    m = mod_ref[...]
    h = _prenorm_mod(x, ng_ref[1:2], m[3:4], m[4:5]).astype(BF16)
    n = x.shape[0]
    xx = _dot(m_ref[...], h)
    y = _channel_mix(xx[:n].astype(BF16), xx[n:].astype(BF16), cd_ref[...], sd_ref[...], w_ref[...], b_ref[...])
    o_ref[...] = x + m[5:6] * y


def _four_ctx(ctx, lay, mseq, cd, sd, w, b, slot):
    nb, n, d = ctx.shape
    tok = pl.BlockSpec((None, n, d), lambda bb: (bb, 0, 0))
    operands = [(tok, ctx)] + _layer_blocks(lay) + [_whole(mseq), _whole(cd), _whole(sd),
                                                    _whole(w, slot), _whole(b, slot)]
    return _call(
        _four_ctx_body,
        operands,
        grid=(nb,),
        out_specs=tok,
        out_shape=jax.ShapeDtypeStruct(ctx.shape, F32),
        compiler_params=_params("parallel"),
        name="four_ctx",
    )


def _dft_tables(n_lat, n_ctx, gd):
    n1 = int(round(n_lat ** 0.5))
    two_pi = 2.0 * np.pi
    i = jnp.arange(n1, dtype=jnp.int32)
    prod = i[:, None] * i[None, :]
    a = (prod % n1).astype(F32) * (two_pi / n1)
    b = prod.astype(F32) * (two_pi / n_lat)
    ca, sa = jnp.cos(a)[None], jnp.sin(a)[None]
    cb, sb = jnp.cos(b).T[:, :, None], jnp.sin(b).T[:, :, None]
    ftab = jnp.concatenate([ca * cb - sa * sb, -(sa * cb + ca * sb)], axis=1).astype(BF16)

    def cs(n, scale):
        i = jnp.arange(n, dtype=jnp.int32)
        a = ((i[:, None] * i[None, :]) % n).astype(F32) * (two_pi / n)
        return jnp.cos(a) * scale, jnp.sin(a) * scale

    c2, s2 = cs(n1, 1.0 / n1)
    m2 = jnp.concatenate([jnp.concatenate([c2, s2], axis=1),
                          jnp.concatenate([-s2, c2], axis=1)], axis=0).astype(BF16)
    cc, sc = cs(n_ctx, n_ctx ** -0.5)
    mctx = jnp.concatenate([cc, -sc], axis=0).astype(BF16)
    cd, sd = cs(gd, gd ** -0.5)
    return ftab, m2, mctx, cd.astype(BF16), sd.astype(BF16)


def _ret_proj_body(*refs, rope, qk):
    if rope:
        x_ref, mod_ref, ng_ref, w_ref, cos_ref, sin_ref, q_ref, k_ref, v_ref, g_ref = refs
    else:
        x_ref, mod_ref, ng_ref, w_ref, q_ref, k_ref, v_ref, g_ref = refs
    m = mod_ref[...]
    h = _prenorm_mod(x_ref[...], ng_ref[1:2], m[3:4], m[4:5]).astype(BF16)
    dk = qk // HEADS
    half = dk // 2
    for src, dst, scale in ((0, q_ref, 1.0), (qk, k_ref, dk ** -0.5)):
        t = _dot(h, w_ref[:, src:src + qk]) * scale
        for s in range(0, qk, dk):
            t1 = t[:, s:s + half]
            t2 = t[:, s + half:s + dk]
            if rope:
                cos = cos_ref[...]
                sin = sin_ref[...]
                t1, t2 = t1 * cos - t2 * sin, t2 * cos + t1 * sin
            dst[:, s:s + half] = t1.astype(BF16)
            dst[:, s + half:s + dk] = t2.astype(BF16)
    vd = v_ref.shape[-1]
    v_ref[...] = _dot(h, w_ref[:, 2 * qk:2 * qk + vd]).astype(BF16)
    g_ref[...] = _silu(_dot(h, w_ref[:, 2 * qk + vd:2 * qk + 2 * vd])).astype(BF16)


def _ret_proj(x, lay, w_in, slot, cos=None, sin=None, *, qk, vd, tm=512):
    nb, t, d = x.shape
    rope = cos is not None
    tok = lambda width: pl.BlockSpec((None, tm, width), lambda b, i: (b, i, 0))
    operands = [(tok(d), x)] + _layer_blocks(lay) + [_whole(w_in, slot)]
    if rope:
        pos = pl.BlockSpec((tm, cos.shape[1]), lambda b, i: (i, 0))
        operands += [(pos, cos), (pos, sin)]
    return _call(
        functools.partial(_ret_proj_body, rope=rope, qk=qk),
        operands,
        grid=(nb, t // tm),
        out_specs=[tok(qk), tok(qk), tok(vd), tok(vd)],
        out_shape=[jax.ShapeDtypeStruct((nb, t, w), BF16) for w in (qk, qk, vd, vd)],
        compiler_params=_params("parallel", "parallel"),
        name="ret_proj",
    )


def _ret_scan_body(*refs, reverse, chunk, finish):
    if finish:
        dec_ref, q_ref, k_ref, v_ref, s0_ref, yf_ref, g_ref, x_ref, mod_ref, wo_ref, y_ref, sfin_ref, s_scr = refs
    else:
        dec_ref, q_ref, k_ref, v_ref, s0_ref, y_ref, sfin_ref, s_scr = refs
    t = pl.program_id(1)

    @pl.when(t == 0)
    def _():
        s_scr[...] = s0_ref[...]

    c = chunk
    n_chunks = q_ref.shape[0] // c
    dk, dv = s_scr.shape[1:]
    row = lax.broadcasted_iota(jnp.int32, (c, c), 0).astype(F32)
    col = lax.broadcasted_iota(jnp.int32, (c, c), 1).astype(F32)
    pos = lax.broadcasted_iota(jnp.int32, (c, 1), 0).astype(F32)
    diff = col - row if reverse else row - col
    decays = []
    for hd in range(HEADS):
        lg = -jnp.abs(dec_ref[hd][:, 0:1])
        if reverse:
            q_dec, k_dec = jnp.exp(lg * (c - pos)), jnp.exp(lg * pos)
        else:
            q_dec, k_dec = jnp.exp(lg * (pos + 1.0)), jnp.exp(lg * (c - 1.0 - pos))
        intra = jnp.where(diff >= 0, jnp.exp(lg * jnp.maximum(diff, 0.0)), 0.0)
        decays.append((q_dec, k_dec, intra, jnp.exp(lg * c)))

    for ci in (range(n_chunks - 1, -1, -1) if reverse else range(n_chunks)):
        sl = pl.ds(ci * c, c)
        gated = []
        for hd in range(HEADS):
            q_dec, k_dec, intra, chunk_dec = decays[hd]
            qs, vs = pl.ds(hd * dk, dk), pl.ds(hd * dv, dv)
            q = q_ref[sl, qs]
            k = k_ref[sl, qs]
            v = v_ref[sl, vs]
            s = s_scr[hd]
            scores = lax.dot_general(q, k, (((1,), (1,)), ((), ())), preferred_element_type=F32) * intra
            y = _dot(scores.astype(BF16), v) + _dot((q.astype(F32) * q_dec).astype(BF16), s.astype(BF16))
            kd = (k.astype(F32) * k_dec).astype(BF16)
            s_scr[hd] = chunk_dec * s + lax.dot_general(kd, v, (((0,), (0,)), ((), ())),
                                                        preferred_element_type=F32)
            if finish:
                y = y + yf_ref[sl, vs]
                mu = jnp.mean(y, axis=-1, keepdims=True)
                yc = y - mu
                yn = yc * lax.rsqrt(jnp.mean(yc * yc, axis=-1, keepdims=True) + EPS)
                gated.append((g_ref[sl, vs].astype(F32) * yn).astype(BF16))
            else:
                y_ref[sl, vs] = y
        if finish:
            mixed = _dot(jnp.concatenate(gated, axis=1), wo_ref[...])
            y_ref[sl, :] = x_ref[sl, :] + mod_ref[5:6] * mixed

    @pl.when(t == pl.num_programs(1) - 1)
    def _():
        sfin_ref[...] = s_scr[...]


def _ret_scan(dec, q, k, v, s0, finish=None, *, reverse, tb, chunk):
    nb, t, qk = q.shape
    vd = v.shape[-1]
    dk, dv = qk // HEADS, vd // HEADS
    nt = t // tb
    tsel = (lambda i: nt - 1 - i) if reverse else (lambda i: i)
    tokspec = lambda w: pl.BlockSpec((None, tb, w), lambda b, i: (b, tsel(i), 0))
    state = pl.BlockSpec((None, HEADS, dk, dv), lambda b, i: (b, 0, 0, 0))
    operands = [_whole(dec), (tokspec(qk), q), (tokspec(qk), k), (tokspec(vd), v), (state, s0)]
    y_shape = jax.ShapeDtypeStruct((nb, t, vd), F32)
    if finish is not None:
        yf, gate, x, lay, w_out, slot = finish
        d = x.shape[-1]
        operands += [(tokspec(vd), yf), (tokspec(vd), gate), (tokspec(d), x), _layer_blocks(lay)[0],
                     _whole(w_out, slot)]
        y_shape = jax.ShapeDtypeStruct(x.shape, F32)
    return _call(
        functools.partial(_ret_scan_body, reverse=reverse, chunk=chunk, finish=finish is not None),
        operands,
        grid=(nb, nt),
        out_specs=[tokspec(y_shape.shape[-1]), state],
        out_shape=[y_shape, jax.ShapeDtypeStruct((nb, HEADS, dk, dv), F32)],
        scratch_shapes=[pltpu.VMEM((HEADS, dk, dv), F32)],
        compiler_params=_params("parallel", "arbitrary"),
        name="ret_scan_bwd" if reverse else "ret_scan_fwd",
    )


def _axial_rope(n_tokens, dk):
    rows = n_tokens // GRID_W
    n_freq = dk // 4
    inv_freq = ROPE_BASE ** (-jnp.arange(n_freq, dtype=F32) / n_freq)
    row_ang = jnp.arange(rows, dtype=F32)[:, None] * inv_freq
    col_ang = jnp.arange(GRID_W, dtype=F32)[:, None] * inv_freq

    def table(fn):
        return jnp.concatenate([jnp.repeat(fn(row_ang), GRID_W, axis=0), jnp.tile(fn(col_ang), (rows, 1))], axis=-1)

    return table(jnp.cos), table(jnp.sin)


def _retention(x, ctx, lat, cx, w_in, w_out, slot, decay, cos, sin):
    nb, n, d = x.shape
    n_ctx = ctx.shape[1]
    qk = d
    vd = w_out.shape[1]
    dk, dv = qk // HEADS, vd // HEADS
    dec = jnp.broadcast_to(decay[:, :, None, None], (2, HEADS, 1, LANES))
    scan = functools.partial(_ret_scan, chunk=RET_CHUNK)

    ctx_flat = ctx.reshape(1, nb * n_ctx, d)
    qc, kc, vc, gc = [a.reshape(nb, n_ctx, -1) for a in _ret_proj(ctx_flat, cx, w_in, slot, qk=qk, vd=vd)]
    zeros = jnp.zeros((nb, HEADS, dk, dv), F32)
    yc_f, s_f = scan(dec[0], qc, kc, vc, zeros, reverse=False, tb=n_ctx)
    ctx, s_b = scan(dec[1], qc, kc, vc, zeros, (yc_f, gc, ctx, cx, w_out, slot), reverse=True, tb=n_ctx)

    ql, kl, vl, gl = _ret_proj(x, lat, w_in, slot, cos, sin, qk=qk, vd=vd)
    yl_f, _ = scan(dec[0], ql, kl, vl, s_f, reverse=False, tb=RET_BLOCK_FWD)
    x, _ = scan(dec[1], ql, kl, vl, s_b, (yl_f, gl, x, lat, w_out, slot), reverse=True, tb=RET_BLOCK_BWD)
    return x, ctx


def kernel(x, c, ctx, c_ctx, ada_w, ada_b, norm_g, final_g, ffn_w_gate, ffn_w_up, ffn_w_down,
           four_w, four_b, ret_w_in, ret_w_out, ret_decay):
    nb, n, d = x.shape
    n_ctx = ctx.shape[1]
    depth = ada_w.shape[0]
    n1 = int(round(n ** 0.5))
    assert n1 * n1 == n and nb + 1 <= MOD_ROWS

    cvec = jnp.zeros((MOD_ROWS, d), F32).at[:nb].set(c).at[nb].set(c_ctx)
    mods = _mods(cvec, ada_w, ada_b).reshape(depth, MOD_ROWS, N_MOD, d)
    w3 = tuple(w.astype(BF16) for w in (ffn_w_gate, ffn_w_up, ffn_w_down))
    four_wb, w_in, w_out = (w.astype(BF16) for w in (four_w, ret_w_in, ret_w_out))
    four_b3 = four_b.reshape(four_b.shape[0], 1, d)
    ftab, m2, mctx, cd, sd = _dft_tables(n, n_ctx, d // N_GROUPS)
    cos, sin = _axial_rope(n, d // HEADS)

    for l in range(depth):
        last = l == depth - 1
        mixer, slot = l % 2, l // 2
        ctx_read = (not last) or mixer == 1
        lat = Layer(mods, norm_g, l, None)
        cx = Layer(mods, norm_g, l, nb)

        x, new_ctx, h = _ffn(x, lat, w3, 0, j=0, ctx=ctx if ctx_read else None, emit_h=mixer == 0)
        ctx = new_ctx if ctx_read else ctx

        y_add = None
        if mixer == 0:
            b1 = _four1(h.reshape(nb, n1, n1, d), ftab)
            y_add = _four2(b1, m2, cd, sd, four_wb, four_b3, slot).reshape(nb, n, d)
            if ctx_read:
                ctx = _four_ctx(ctx, cx, mctx, cd, sd, four_wb, four_b3, slot)
        else:
            x, new_ctx = _retention(x, ctx, lat, cx, w_in, w_out, slot, ret_decay[slot], cos, sin)
            ctx = ctx if last else new_ctx

        x, new_ctx, _ = _ffn(x, lat, w3, 1, j=2, ctx=None if last else ctx, add=y_add,
                             final_g=final_g if last else None)
        ctx = ctx if last else new_ctx
    return x
```
